```python
import math, functools
import jax, jax.numpy as jnp
from jax import lax
import numpy as np


D_MODEL = 1024
BATCH = 4
SEQ = 4096
DEPTH = 4
DEC_BATCH = 32
DEC_SEQ = 16
PAST_LEN = 4096

CHUNK = 64
Q_BLOCK = 128
N_MIXERS = 2
N_FOX = (DEPTH + N_MIXERS - 1) // N_MIXERS
N_DIFF = DEPTH // N_MIXERS
FOX_HEADS = 16
FOX_HEAD_DIM = 64
FOX_WIDTH = FOX_HEADS * FOX_HEAD_DIM
FOX_IN_WIDTH = 4 * FOX_WIDTH + FOX_HEADS
DIFF_HEADS = 8
DIFF_HEAD_DIM = 64
DIFF_QK_WIDTH = DIFF_HEADS * 2 * DIFF_HEAD_DIM
DIFF_V_DIM = 2 * DIFF_HEAD_DIM
DIFF_IN_WIDTH = 2 * DIFF_QK_WIDTH + DIFF_HEADS * DIFF_V_DIM
PEER_HEADS = 8
PEER_QUERY_DIM = 256
PEER_HALF_DIM = PEER_QUERY_DIM // 2
N_KEYS = 128
N_EXPERTS = N_KEYS * N_KEYS
PEER_TOPK = 16
PEER_BLOCK = 128
ROPE_THETA = 10000.0
EPS = 1e-6
NEG_INF = -1e30
FOX_SCALE = 1.0 / math.sqrt(FOX_HEAD_DIM)
DIFF_SCALE = 1.0 / math.sqrt(DIFF_HEAD_DIM)

kernel_name = "fox_diff_peer_streaming_step"


def rmsnorm(x, g):
    xf = x.astype(jnp.float32)
    y = xf * lax.rsqrt(jnp.mean(xf * xf, axis=-1, keepdims=True) + EPS)
    return (y * g.astype(jnp.float32)).astype(x.dtype)


def apply_rope(x, pos):
    d = x.shape[-1]
    inv = ROPE_THETA ** (-jnp.arange(0, d, 2, dtype=jnp.float32) / d)
    ang = pos.astype(jnp.float32)[:, None] * inv[None, :]
    ang = jnp.concatenate([ang, ang], axis=-1)
    shape = (1, pos.shape[0]) + (1,) * (x.ndim - 3) + (d,)
    cos = jnp.cos(ang).reshape(shape)
    sin = jnp.sin(ang).reshape(shape)
    xf = x.astype(jnp.float32)
    rot = jnp.concatenate([-xf[..., d // 2:], xf[..., : d // 2]], axis=-1)
    return (xf * cos + rot * sin).astype(x.dtype)


def sweep_query_blocks(attend, q_inputs, q_pos, kv_inputs):
    nblk = q_pos.shape[0] // Q_BLOCK

    def split(a):
        return jnp.swapaxes(a.reshape(a.shape[0], nblk, Q_BLOCK, *a.shape[2:]), 0, 1)

    blocks = tuple(split(a) for a in q_inputs) + (q_pos.reshape(nblk, Q_BLOCK),)
    out = lax.map(lambda blk: attend(*blk, *kv_inputs), blocks)
    out = jnp.swapaxes(out, 0, 1)
    return out.reshape(out.shape[0], nblk * Q_BLOCK, *out.shape[3:])


def fox_project(h, w_in, b_f, q_g, k_g):
    B, T, _ = h.shape
    p = h @ w_in
    q = rmsnorm(p[..., :FOX_WIDTH].reshape(B, T, FOX_HEADS, FOX_HEAD_DIM), q_g)
    k = rmsnorm(p[..., FOX_WIDTH:2 * FOX_WIDTH].reshape(B, T, FOX_HEADS, FOX_HEAD_DIM), k_g)
    v = p[..., 2 * FOX_WIDTH:3 * FOX_WIDTH].reshape(B, T, FOX_HEADS, FOX_HEAD_DIM)
    gate = p[..., 3 * FOX_WIDTH:4 * FOX_WIDTH]
    logf = jax.nn.log_sigmoid(p[..., 4 * FOX_WIDTH:].astype(jnp.float32) + b_f.astype(jnp.float32))
    return q, k, v, gate, logf


def fox_attend(q, cq, q_pos, k, v, ck, k_pos):
    s = jnp.einsum('bqhd,bkhd->bhqk', q, k).astype(jnp.float32) * FOX_SCALE
    s = s + jnp.swapaxes(cq, 1, 2)[..., :, None] - jnp.swapaxes(ck, 1, 2)[..., None, :]
    mask = k_pos[None, :] <= q_pos[:, None]
    p = jax.nn.softmax(jnp.where(mask, s, NEG_INF), axis=-1)
    return jnp.einsum('bhqk,bkhd->bqhd', p.astype(v.dtype), v)


def fox_output(o, gate, w_o):
    B, T = o.shape[:2]
    return (o.reshape(B, T, FOX_WIDTH) * jax.nn.sigmoid(gate)) @ w_o


def diff_project(h, pos, w_in, q_g, k_g):
    B, T, _ = h.shape
    p = h @ w_in
    q = p[..., :DIFF_QK_WIDTH].reshape(B, T, DIFF_HEADS, 2, DIFF_HEAD_DIM)
    k = p[..., DIFF_QK_WIDTH:2 * DIFF_QK_WIDTH].reshape(B, T, DIFF_HEADS, 2, DIFF_HEAD_DIM)
    v = p[..., 2 * DIFF_QK_WIDTH:].reshape(B, T, DIFF_HEADS, DIFF_V_DIM)
    q = apply_rope(rmsnorm(q, q_g), pos)
    k = apply_rope(rmsnorm(k, k_g), pos)
    return q, k, v


def diff_lambda_value(lam_vecs, lam_init):
    lf = lam_vecs.astype(jnp.float32)
    return jnp.exp(jnp.sum(lf[0] * lf[1])) - jnp.exp(jnp.sum(lf[2] * lf[3])) + lam_init


def diff_attend(q, q_pos, k, v, k_pos, lam):
    s = jnp.einsum('bqhnd,bkhnd->bhnqk', q, k).astype(jnp.float32) * DIFF_SCALE
    mask = (k_pos[None, :] // CHUNK) <= (q_pos[:, None] // CHUNK)
    p = jax.nn.softmax(jnp.where(mask, s, NEG_INF), axis=-1)
    a = p[:, :, 0] - lam * p[:, :, 1]
    return jnp.einsum('bhqk,bkhe->bqhe', a.astype(v.dtype), v)


def diff_output(o, sub_g, lam_init, w_o):
    B, T = o.shape[:2]
    o = rmsnorm(o, sub_g) * (1.0 - lam_init)
    return o.reshape(B, T, DIFF_HEADS * DIFF_V_DIM) @ w_o


def peer_tokens(xt, w_q, sub_keys, u, v):
    T = xt.shape[0]
    q = (xt @ w_q).reshape(T, PEER_HEADS, 2, PEER_HALF_DIM)
    s = jnp.einsum('thcd,hcnd->thcn', q, sub_keys).astype(jnp.float32)
    sv, si = lax.top_k(s, PEER_TOPK)
    cand = sv[:, :, 0, :, None] + sv[:, :, 1, None, :]
    cidx = si[:, :, 0, :, None] * N_KEYS + si[:, :, 1, None, :]
    cand = cand.reshape(T, PEER_HEADS, PEER_TOPK * PEER_TOPK)
    cidx = cidx.reshape(T, PEER_HEADS, PEER_TOPK * PEER_TOPK)
    top_v, top_pos = lax.top_k(cand, PEER_TOPK)
    idx = jnp.take_along_axis(cidx, top_pos, axis=-1)
    gate = jax.nn.softmax(top_v, axis=-1)
    act = jax.nn.gelu(jnp.einsum('thkd,td->thk', u[idx], xt).astype(jnp.float32), approximate=False)
    return jnp.einsum('thk,thkd->td', (gate * act).astype(xt.dtype), v[idx])


def peer_prompt(h, w_q, sub_keys, u, v):
    blocks = h.reshape(-1, PEER_BLOCK, h.shape[-1])
    y = lax.map(lambda xb: peer_tokens(xb, w_q, sub_keys, u, v), blocks)
    return y.reshape(h.shape)


def peer_sample(h, w_q, sub_keys, u, v):
    return peer_tokens(h.reshape(-1, h.shape[-1]), w_q, sub_keys, u, v).reshape(h.shape)


def setup_inputs(seed: int = 0) -> dict:
    key = jax.random.key(seed)
    ks = jax.random.split(key, 26)
    f32 = jnp.float32

    def nrm(k, shape, scale=1.0):
        return jax.random.normal(k, shape, f32) * scale

    inv_d = D_MODEL ** -0.5
    return {
        "x_prompt": nrm(ks[0], (BATCH, SEQ, D_MODEL)),
        "x_sample": nrm(ks[1], (DEC_BATCH, DEC_SEQ, D_MODEL)),
        "cache_fox_k": nrm(ks[2], (N_FOX, DEC_BATCH, PAST_LEN, FOX_HEADS, FOX_HEAD_DIM)),
        "cache_fox_v": nrm(ks[3], (N_FOX, DEC_BATCH, PAST_LEN, FOX_HEADS, FOX_HEAD_DIM)),
        "cache_fox_logf": jax.nn.log_sigmoid(nrm(ks[4], (N_FOX, DEC_BATCH, PAST_LEN, FOX_HEADS)) + 3.0),
        "cache_diff_k": nrm(ks[5], (N_DIFF, DEC_BATCH, PAST_LEN, DIFF_HEADS, 2, DIFF_HEAD_DIM)),
        "cache_diff_v": nrm(ks[6], (N_DIFF, DEC_BATCH, PAST_LEN, DIFF_HEADS, DIFF_V_DIM)),
        "norm_mix_g": 1.0 + nrm(ks[7], (DEPTH, D_MODEL), 0.01),
        "norm_ffn_g": 1.0 + nrm(ks[8], (DEPTH, D_MODEL), 0.01),
        "fox_w_in": nrm(ks[9], (N_FOX, D_MODEL, FOX_IN_WIDTH), inv_d),
        "fox_b_f": jax.random.uniform(ks[10], (N_FOX, FOX_HEADS), f32, 1.0, 5.0),
        "fox_q_g": 1.0 + nrm(ks[11], (N_FOX, FOX_HEAD_DIM), 0.01),
        "fox_k_g": 1.0 + nrm(ks[12], (N_FOX, FOX_HEAD_DIM), 0.01),
        "fox_w_o": nrm(ks[13], (N_FOX, FOX_WIDTH, D_MODEL), FOX_WIDTH ** -0.5),
        "diff_w_in": nrm(ks[14], (N_DIFF, D_MODEL, DIFF_IN_WIDTH), inv_d),
        "diff_q_g": 1.0 + nrm(ks[15], (N_DIFF, DIFF_HEAD_DIM), 0.01),
        "diff_k_g": 1.0 + nrm(ks[16], (N_DIFF, DIFF_HEAD_DIM), 0.01),
        "diff_lambda": nrm(ks[17], (N_DIFF, 4, DIFF_HEAD_DIM), 0.1),
        "diff_sub_g": 1.0 + nrm(ks[18], (N_DIFF, DIFF_V_DIM), 0.01),
        "diff_w_o": nrm(ks[19], (N_DIFF, DIFF_HEADS * DIFF_V_DIM, D_MODEL), (DIFF_HEADS * DIFF_V_DIM) ** -0.5),
        "peer_w_q": nrm(ks[20], (DEPTH, D_MODEL, PEER_HEADS * PEER_QUERY_DIM), inv_d),
        "peer_sub_keys": nrm(ks[21], (DEPTH, PEER_HEADS, 2, N_KEYS, PEER_HALF_DIM), PEER_HALF_DIM ** -0.5),
        "peer_u": nrm(ks[22], (DEPTH, N_EXPERTS, D_MODEL), inv_d),
        "peer_v": nrm(ks[23], (DEPTH, N_EXPERTS, D_MODEL), 0.5 * PEER_HEADS ** -0.5),
    }


def reference(x_prompt, x_sample, cache_fox_k, cache_fox_v, cache_fox_logf, cache_diff_k, cache_diff_v,
              norm_mix_g, norm_ffn_g, fox_w_in, fox_b_f, fox_q_g, fox_k_g, fox_w_o,
              diff_w_in, diff_q_g, diff_k_g, diff_lambda, diff_sub_g, diff_w_o,
              peer_w_q, peer_sub_keys, peer_u, peer_v):
    xp, xs = x_prompt, x_sample
    S = xp.shape[1]
    T = xs.shape[1]
    past = cache_fox_k.shape[2]
    pos_p = jnp.arange(S, dtype=jnp.int32)
    pos_all = jnp.arange(past + T, dtype=jnp.int32)
    pos_s = pos_all[past:]

    fk_p, fv_p, fl_p, dk_p, dv_p = [], [], [], [], []
    fk_s, fv_s, fl_s, dk_s, dv_s = [], [], [], [], []

    for layer in range(DEPTH):
        j = layer // N_MIXERS
        hp = rmsnorm(xp, norm_mix_g[layer])
        hs = rmsnorm(xs, norm_mix_g[layer])
        if layer % N_MIXERS == 0:
            q, k, v, gate, logf = fox_project(hp, fox_w_in[j], fox_b_f[j], fox_q_g[j], fox_k_g[j])
            c = jnp.cumsum(logf, axis=1)
            o = sweep_query_blocks(fox_attend, (q, c), pos_p, (k, v, c, pos_p))
            yp = fox_output(o, gate, fox_w_o[j])
            fk_p.append(k); fv_p.append(v); fl_p.append(logf.astype(xp.dtype))

            qs, ks_, vs, gs, lfs = fox_project(hs, fox_w_in[j], fox_b_f[j], fox_q_g[j], fox_k_g[j])
            k_all = jnp.concatenate([cache_fox_k[j], ks_], axis=1)
            v_all = jnp.concatenate([cache_fox_v[j], vs], axis=1)
            lf_all = jnp.concatenate([cache_fox_logf[j].astype(jnp.float32), lfs], axis=1)
            c_all = jnp.cumsum(lf_all, axis=1)
            o_s = fox_attend(qs, c_all[:, past:], pos_s, k_all, v_all, c_all, pos_all)
            ys = fox_output(o_s, gs, fox_w_o[j])
            fk_s.append(ks_); fv_s.append(vs); fl_s.append(lfs.astype(xs.dtype))
        else:
            lam_init = 0.8 - 0.6 * math.exp(-0.3 * layer)
            lam = diff_lambda_value(diff_lambda[j], lam_init)
            attend = functools.partial(diff_attend, lam=lam)
            q, k, v = diff_project(hp, pos_p, diff_w_in[j], diff_q_g[j], diff_k_g[j])
            o = sweep_query_blocks(attend, (q,), pos_p, (k, v, pos_p))
            yp = diff_output(o, diff_sub_g[j], lam_init, diff_w_o[j])
            dk_p.append(k); dv_p.append(v)

            qs, ks_, vs = diff_project(hs, pos_s, diff_w_in[j], diff_q_g[j], diff_k_g[j])
            k_all = jnp.concatenate([cache_diff_k[j], ks_], axis=1)
            v_all = jnp.concatenate([cache_diff_v[j], vs], axis=1)
            o_s = attend(qs, pos_s, k_all, v_all, pos_all)
            ys = diff_output(o_s, diff_sub_g[j], lam_init, diff_w_o[j])
            dk_s.append(ks_); dv_s.append(vs)
        xp = xp + yp
        xs = xs + ys

        hp = rmsnorm(xp, norm_ffn_g[layer])
        hs = rmsnorm(xs, norm_ffn_g[layer])
        xp = xp + peer_prompt(hp, peer_w_q[layer], peer_sub_keys[layer], peer_u[layer], peer_v[layer])
        xs = xs + peer_sample(hs, peer_w_q[layer], peer_sub_keys[layer], peer_u[layer], peer_v[layer])

    return (xp, xs,
            jnp.stack(fk_p), jnp.stack(fv_p), jnp.stack(fl_p), jnp.stack(dk_p), jnp.stack(dv_p),
            jnp.stack(fk_s), jnp.stack(fv_s), jnp.stack(fl_s), jnp.stack(dk_s), jnp.stack(dv_s))
```

```python
import functools
import math

import jax
import jax.numpy as jnp
from jax import lax
from jax.experimental import pallas as pl
from jax.experimental.pallas import tpu as pltpu

F32 = jnp.float32
BF16 = jnp.bfloat16
EPS = 1e-6
NEG_INF = -1e30
CHUNK = 64
ROPE_THETA = 10000.0
PEER_TOPK = 16
LANES = 128
HEAD_DIM = 64
VMEM_LIMIT = 56 * 1024 * 1024

_NT = (((1,), (1,)), ((), ()))


def _params(sem, vmem=VMEM_LIMIT):
    return pltpu.CompilerParams(dimension_semantics=sem, vmem_limit_bytes=vmem)


def _norm_matmul_kernel(x_ref, g_ref, w_ref, o_ref, hn_ref):
    @pl.when(pl.program_id(1) == 0)
    def _():
        x = x_ref[...]
        y = x * lax.rsqrt(jnp.mean(x * x, axis=-1, keepdims=True) + EPS)
        hn_ref[...] = (y * g_ref[...]).astype(BF16)

    o_ref[...] = jnp.dot(hn_ref[...], w_ref[...], preferred_element_type=F32)


def norm_matmul(x, g, w, tm, tn):
    M, D = x.shape
    N = w.shape[1]
    return pl.pallas_call(
        _norm_matmul_kernel,
        grid=(M // tm, N // tn),
        in_specs=[pl.BlockSpec((tm, D), lambda i, j: (i, 0)),
                  pl.BlockSpec((1, D), lambda i, j: (0, 0)),
                  pl.BlockSpec((D, tn), lambda i, j: (0, j))],
        out_specs=pl.BlockSpec((tm, tn), lambda i, j: (i, j)),
        out_shape=jax.ShapeDtypeStruct((M, N), F32),
        scratch_shapes=[pltpu.VMEM((tm, D), BF16)],
        compiler_params=_params(("arbitrary", "arbitrary")),
        name="norm_matmul",
    )(x, g, w)


def _matmul_res_kernel(a_ref, w_ref, r_ref, o_ref):
    o_ref[...] = r_ref[...] + jnp.dot(a_ref[...].astype(BF16), w_ref[...],
                                      preferred_element_type=F32)


def matmul_residual(a, w, res, tm):
    M, K = a.shape
    N = w.shape[1]
    return pl.pallas_call(
        _matmul_res_kernel,
        grid=(M // tm,),
        in_specs=[pl.BlockSpec((tm, K), lambda i: (i, 0)),
                  pl.BlockSpec((K, N), lambda i: (0, 0)),
                  pl.BlockSpec((tm, N), lambda i: (i, 0))],
        out_specs=pl.BlockSpec((tm, N), lambda i: (i, 0)),
        out_shape=jax.ShapeDtypeStruct((M, N), F32),
        compiler_params=_params(("arbitrary",)),
        name="matmul_residual",
    )(a, w, res)


def _head_rms64(xc, lo):
    sq = xc * xc
    s_lo = jnp.sum(jnp.where(lo, sq, 0.0), axis=-1, keepdims=True)
    s_hi = jnp.sum(jnp.where(lo, 0.0, sq), axis=-1, keepdims=True)
    ms = jnp.where(lo, s_lo, s_hi) * (1.0 / HEAD_DIM)
    return lax.rsqrt(ms + EPS)


def _fox_post_kernel(q_ref, k_ref, f_ref, bf_ref, qg_ref, kg_ref, qo_ref, ko_ref, lf_ref):
    tm, width = q_ref.shape
    lo = lax.broadcasted_iota(jnp.int32, (tm, LANES), 1) < HEAD_DIM
    for src, g_ref, dst in ((q_ref, qg_ref, qo_ref), (k_ref, kg_ref, ko_ref)):
        for c in range(width // LANES):
            sl = slice(c * LANES, (c + 1) * LANES)
            xc = src[:, sl]
            dst[:, sl] = xc * _head_rms64(xc, lo) * g_ref[:, sl]
    z = f_ref[...] + bf_ref[...]
    lf_ref[...] = jnp.minimum(z, 0.0) - jnp.log1p(jnp.exp(-jnp.abs(z)))


def fox_post(p, pf, b_f, q_g, k_g, tm):
    M = p.shape[0]
    W = q_g.shape[1]
    row = lambda c: pl.BlockSpec((tm, W), lambda i, c=c: (i, c))
    vec = lambda n: pl.BlockSpec((1, n), lambda i: (0, 0))
    return pl.pallas_call(
        _fox_post_kernel,
        grid=(M // tm,),
        in_specs=[row(0), row(1), pl.BlockSpec((tm, LANES), lambda i: (i, 0)),
                  vec(LANES), vec(W), vec(W)],
        out_specs=[row(0), row(0), pl.BlockSpec((tm, LANES), lambda i: (i, 0))],
        out_shape=[jax.ShapeDtypeStruct((M, W), F32), jax.ShapeDtypeStruct((M, W), F32),
                   jax.ShapeDtypeStruct((M, LANES), F32)],
        compiler_params=_params(("arbitrary",)),
        name="fox_post",
    )(p, p, pf, b_f, q_g, k_g)


def _diff_post_kernel(q_ref, k_ref, cos_ref, sin_ref, qg_ref, kg_ref, qo_ref, ko_ref):
    tm, width = q_ref.shape
    lane = lax.broadcasted_iota(jnp.int32, (tm, LANES), 1)
    lo = lane < HEAD_DIM
    first_half = (lane % HEAD_DIM) < (HEAD_DIM // 2)
    cos = cos_ref[...]
    sin = sin_ref[...]
    for src, g_ref, dst in ((q_ref, qg_ref, qo_ref), (k_ref, kg_ref, ko_ref)):
        for c in range(width // LANES):
            sl = slice(c * LANES, (c + 1) * LANES)
            xc = src[:, sl]
            xn = xc * _head_rms64(xc, lo) * g_ref[:, sl]
            rot = jnp.where(first_half,
                            -pltpu.roll(xn, LANES - HEAD_DIM // 2, axis=1),
                            pltpu.roll(xn, HEAD_DIM // 2, axis=1))
            dst[:, sl] = xn * cos + rot * sin


def diff_post(p, cos, sin, q_g, k_g, tm):
    M = p.shape[0]
    W = q_g.shape[1]
    row = lambda c: pl.BlockSpec((tm, W), lambda i, c=c: (i, c))
    tab = pl.BlockSpec((tm, LANES), lambda i: (i, 0))
    vec = pl.BlockSpec((1, W), lambda i: (0, 0))
    return pl.pallas_call(
        _diff_post_kernel,
        grid=(M // tm,),
        in_specs=[row(0), row(1), tab, tab, vec, vec],
        out_specs=[row(0), row(0)],
        out_shape=[jax.ShapeDtypeStruct((M, W), F32), jax.ShapeDtypeStruct((M, W), F32)],
        compiler_params=_params(("arbitrary",)),
        name="diff_post",
    )(p, p, cos, sin, q_g, k_g)


def _diff_lambda(lam_ref, lam_init):
    lf = lam_ref[...]
    a = jnp.sum(lf[0:1, :] * lf[1:2, :], axis=-1, keepdims=True)
    b = jnp.sum(lf[2:3, :] * lf[3:4, :], axis=-1, keepdims=True)
    return jnp.exp(a) - jnp.exp(b) + lam_init


def _online_softmax_step(s, v, m_ref, l_ref, acc_ref):
    m_prev = m_ref[...]
    m_new = jnp.maximum(m_prev, jnp.max(s, axis=-1, keepdims=True))
    alpha = jnp.exp(m_prev - m_new)
    p = jnp.exp(s - m_new)
    l_ref[...] = alpha * l_ref[...] + jnp.sum(p, axis=-1, keepdims=True)
    acc_ref[...] = alpha * acc_ref[...] + jnp.dot(p.astype(BF16), v, preferred_element_type=F32)
    m_ref[...] = m_new


def _prompt_attn_kernel(qi_tab, ki_tab, q_ref, k_ref, v_ref, *rest, mode, lam_init, scale):
    if mode == "fox":
        cq_ref, ck_ref, gate_ref, o_ref = rest[:4]
        scratch = rest[4:]
    else:
        lam_ref, subg_ref, o_ref = rest[:3]
        scratch = rest[3:]
    q0_ref, q1_ref, m0_ref, m1_ref, l0_ref, l1_ref, acc0_ref, acc1_ref = scratch
    tq = q_ref.shape[0]
    tk = k_ref.shape[0]
    t = pl.program_id(2)
    qi = qi_tab[t]
    ki = ki_tab[t]
    lo_q = lax.broadcasted_iota(jnp.int32, (tq, LANES), 1) < HEAD_DIM

    @pl.when(ki == 0)
    def _():
        q = q_ref[...] * scale
        q0_ref[...] = jnp.where(lo_q, q, 0.0).astype(BF16)
        q1_ref[...] = jnp.where(lo_q, 0.0, q).astype(BF16)
        for m_ref, l_ref, acc_ref in ((m0_ref, l0_ref, acc0_ref), (m1_ref, l1_ref, acc1_ref)):
            m_ref[...] = jnp.full(m_ref.shape, NEG_INF, F32)
            l_ref[...] = jnp.zeros(l_ref.shape, F32)
            acc_ref[...] = jnp.zeros(acc_ref.shape, F32)

    def step(diagonal):
        k = k_ref[...].astype(BF16)
        v = v_ref[...].astype(BF16)
        s0 = lax.dot_general(q0_ref[...], k, _NT, preferred_element_type=F32)
        s1 = lax.dot_general(q1_ref[...], k, _NT, preferred_element_type=F32)
        if mode == "fox":
            s0 = s0 + cq_ref[:, 0:1] - ck_ref[0:1, :]
            s1 = s1 + cq_ref[:, 1:2] - ck_ref[1:2, :]
        if diagonal:
            row = lax.broadcasted_iota(jnp.int32, (tq, tk), 0)
            col = lax.broadcasted_iota(jnp.int32, (tq, tk), 1)
            if mode == "fox":
                visible = col <= row
            else:
                visible = (col // CHUNK) <= (row // CHUNK)
            s0 = jnp.where(visible, s0, NEG_INF)
            s1 = jnp.where(visible, s1, NEG_INF)
        _online_softmax_step(s0, v, m0_ref, l0_ref, acc0_ref)
        _online_softmax_step(s1, v, m1_ref, l1_ref, acc1_ref)

    @pl.when(ki < qi)
    def _():
        step(False)

    @pl.when(ki == qi)
    def _():
        step(True)
        o0 = acc0_ref[...] / l0_ref[...]
        o1 = acc1_ref[...] / l1_ref[...]
        if mode == "fox":
            o_ref[...] = jnp.where(lo_q, o0, o1) * jax.nn.sigmoid(gate_ref[...])
        else:
            o = o0 - _diff_lambda(lam_ref, lam_init) * o1
            y = o * lax.rsqrt(jnp.mean(o * o, axis=-1, keepdims=True) + EPS)
            o_ref[...] = (y * subg_ref[...]) * (1.0 - lam_init)


def prompt_attention(mode, q, k, vsrc, v_col0, batch, seq, tq, *, cq=None, ck=None,
                     gate_src=None, gate_col0=None, lam=None, sub_g=None, lam_init=0.0):
    W = q.shape[1]
    npair = W // LANES
    nq = seq // tq
    tri = [(a, b) for a in range(nq) for b in range(a + 1)]
    qi_tab = jnp.asarray([a for a, _ in tri], jnp.int32)
    ki_tab = jnp.asarray([b for _, b in tri], jnp.int32)
    qmap = lambda b, h, t, qt, kt: (b * nq + qt[t], h)
    kmap = lambda b, h, t, qt, kt: (b * nq + kt[t], h)
    in_specs = [pl.BlockSpec((tq, LANES), qmap),
                pl.BlockSpec((tq, LANES), kmap),
                pl.BlockSpec((tq, LANES), lambda b, h, t, qt, kt: (b * nq + kt[t], v_col0 + h))]
    args = [q, k, vsrc]
    if mode == "fox":
        in_specs += [pl.BlockSpec((None, None, tq, 2), lambda b, h, t, qt, kt: (b, h, qt[t], 0)),
                     pl.BlockSpec((None, None, 2, tq), lambda b, h, t, qt, kt: (b, h, 0, kt[t])),
                     pl.BlockSpec((tq, LANES),
                                  lambda b, h, t, qt, kt: (b * nq + qt[t], gate_col0 + h))]
        args += [cq, ck, gate_src]
    else:
        in_specs += [pl.BlockSpec(lam.shape, lambda b, h, t, qt, kt: (0, 0)),
                     pl.BlockSpec((1, LANES), lambda b, h, t, qt, kt: (0, 0))]
        args += [lam, sub_g]
    stat = pltpu.VMEM((tq, 1), F32)
    acc = pltpu.VMEM((tq, LANES), F32)
    grid_spec = pltpu.PrefetchScalarGridSpec(
        num_scalar_prefetch=2,
        grid=(batch, npair, len(tri)),
        in_specs=in_specs,
        out_specs=pl.BlockSpec((tq, LANES), qmap),
        scratch_shapes=[pltpu.VMEM((tq, LANES), BF16), pltpu.VMEM((tq, LANES), BF16),
                        stat, stat, stat, stat, acc, acc],
    )
    return pl.pallas_call(
        functools.partial(_prompt_attn_kernel, mode=mode, lam_init=lam_init,
                          scale=1.0 / math.sqrt(HEAD_DIM)),
        grid_spec=grid_spec,
        out_shape=jax.ShapeDtypeStruct((batch * seq, W), F32),
        compiler_params=_params(("arbitrary", "arbitrary", "arbitrary")),
        name=f"prompt_attention_{mode}",
    )(qi_tab, ki_tab, *args)


def _sample_attn_kernel(q_ref, kc_ref, vc_ref, kn_ref, vn_ref, *rest, mode, lam_init, scale,
                        past):
    if mode == "fox":
        cq_ref, ckc_ref, ckn_ref, gate_ref, o_ref = rest[:5]
        scratch = rest[5:]
    else:
        lam_ref, subg_ref, o_ref = rest[:3]
        scratch = rest[3:]
    qbd_ref, m_ref, l_ref, acc_ref = scratch
    tnew, W = q_ref.shape
    nmap = W // HEAD_DIM
    R = nmap * tnew
    kstep = pl.program_id(1)
    nk = pl.num_programs(1) - 1

    @pl.when(kstep == 0)
    def _():
        q = q_ref[...] * scale
        qt = jnp.concatenate([q] * nmap, axis=0)
        rmap = lax.broadcasted_iota(jnp.int32, (R, W), 0) // tnew
        cmap = lax.broadcasted_iota(jnp.int32, (R, W), 1) // HEAD_DIM
        qbd_ref[...] = jnp.where(rmap == cmap, qt, 0.0).astype(BF16)
        m_ref[...] = jnp.full(m_ref.shape, NEG_INF, F32)
        l_ref[...] = jnp.zeros(l_ref.shape, F32)
        acc_ref[...] = jnp.zeros(acc_ref.shape, F32)

    def expand_rows(c):
        n = c.shape[1]
        return jnp.broadcast_to(c[:, None, :], (nmap, tnew, n)).reshape(R, n)

    def update(s, v):
        _online_softmax_step(s, v, m_ref, l_ref, acc_ref)

    @pl.when(kstep < nk)
    def _():
        k = kc_ref[...].astype(BF16)
        v = vc_ref[...].astype(BF16)
        s = lax.dot_general(qbd_ref[...], k, _NT, preferred_element_type=F32)
        if mode == "fox":
            s = s + cq_ref[...] - expand_rows(ckc_ref[...])
        update(s, v)

    @pl.when(kstep == nk)
    def _():
        npad = LANES - tnew
        k = jnp.concatenate([kn_ref[...], jnp.zeros((npad, W), F32)], axis=0).astype(BF16)
        v = jnp.concatenate([vn_ref[...], jnp.zeros((npad, W), F32)], axis=0).astype(BF16)
        s = lax.dot_general(qbd_ref[...], k, _NT, preferred_element_type=F32)
        if mode == "fox":
            s = s + cq_ref[...] - expand_rows(ckn_ref[...])
        tpos = lax.broadcasted_iota(jnp.int32, (R, LANES), 0) % tnew
        spos = lax.broadcasted_iota(jnp.int32, (R, LANES), 1)
        if mode == "fox":
            visible = spos <= tpos
        else:
            visible = ((past + spos) // CHUNK) <= ((past + tpos) // CHUNK)
        visible = jnp.logical_and(visible, spos < tnew)
        update(jnp.where(visible, s, NEG_INF), v)

        accn = acc_ref[...] / l_ref[...]
        if mode == "fox":
            lane_map = lax.broadcasted_iota(jnp.int32, (tnew, W), 1) // HEAD_DIM
            o = jnp.zeros((tnew, W), F32)
            for r in range(nmap):
                o = jnp.where(lane_map == r, accn[r * tnew:(r + 1) * tnew, :], o)
            o_ref[...] = o * jax.nn.sigmoid(gate_ref[...])
        else:
            lam = _diff_lambda(lam_ref, lam_init)
            for h in range(nmap // 2):
                sl = slice(h * LANES, (h + 1) * LANES)
                o = (accn[(2 * h) * tnew:(2 * h + 1) * tnew, sl]
                     - lam * accn[(2 * h + 1) * tnew:(2 * h + 2) * tnew, sl])
                y = o * lax.rsqrt(jnp.mean(o * o, axis=-1, keepdims=True) + EPS)
                o_ref[:, sl] = (y * subg_ref[...]) * (1.0 - lam_init)


def sample_attention(mode, q, k, vsrc, v_colblk, kcache, vcache, layer_idx, nstream, tnew,
                     row0, tk, *, cq=None, ck_cache=None, ck_new=None, gate_src=None,
                     gate_colblk=None, lam=None, sub_g=None, lam_init=0.0):
    W = q.shape[1]
    past = kcache.shape[1]
    nk = past // tk
    nmap = W // HEAD_DIM
    R = nmap * tnew
    rb0 = row0 // tnew
    wblk = lambda colblk: pl.BlockSpec((tnew, W), lambda b, s, c=colblk: (rb0 + b, c))
    cache = pl.BlockSpec((None, tk, W),
                         lambda b, s: (layer_idx * nstream + b, jnp.minimum(s, nk - 1), 0))
    in_specs = [wblk(0), cache, cache, wblk(0), wblk(v_colblk)]
    args = [q, kcache, vcache, k, vsrc]
    if mode == "fox":
        in_specs += [pl.BlockSpec((None, R, 1), lambda b, s: (b, 0, 0)),
                     pl.BlockSpec((None, nmap, tk), lambda b, s: (b, 0, jnp.minimum(s, nk - 1))),
                     pl.BlockSpec((None, nmap, LANES), lambda b, s: (b, 0, 0)),
                     wblk(gate_colblk)]
        args += [cq, ck_cache, ck_new, gate_src]
    else:
        in_specs += [pl.BlockSpec(lam.shape, lambda b, s: (0, 0)),
                     pl.BlockSpec((1, LANES), lambda b, s: (0, 0))]
        args += [lam, sub_g]
    return pl.pallas_call(
        functools.partial(_sample_attn_kernel, mode=mode, lam_init=lam_init,
                          scale=1.0 / math.sqrt(HEAD_DIM), past=past),
        grid=(nstream, nk + 1),
        in_specs=in_specs,
        out_specs=pl.BlockSpec((tnew, W), lambda b, s: (b, 0)),
        out_shape=jax.ShapeDtypeStruct((nstream * tnew, W), F32),
        scratch_shapes=[pltpu.VMEM((R, W), BF16), pltpu.VMEM((R, 1), F32),
                        pltpu.VMEM((R, 1), F32), pltpu.VMEM((R, W), F32)],
        compiler_params=_params(("arbitrary", "arbitrary")),
        name=f"sample_attention_{mode}",
    )(*args)


def _top16_desc(e):
    rows = []
    for _ in range(PEER_TOPK):
        m = jnp.max(e, axis=0, keepdims=True)
        rows.append(m)
        e = jnp.where(e == m, -1.0, e)
    return jnp.concatenate(rows, axis=0)


def _pair_candidates(a, b):
    tb = a.shape[1]
    row = lax.broadcasted_iota(jnp.int32, (8, tb), 0)
    b8 = b[0:8, :]
    parts = [b * a[0:1, :], b8 * a[1:2, :]]
    for r1 in range(2, 8):
        parts.append(jnp.where(row < PEER_TOPK // (r1 + 1), b8 * a[r1:r1 + 1, :], -1.0))
    parts.append(b[0:1, :] * a[8:16, :])
    return jnp.concatenate(parts, axis=0)


def _peer_kernel(x_ref, g_ref, wq_ref, sk_ref, u_ref, vt_ref, o_ref,
                 hn_ref, qt_ref, e1_ref, e2_ref, thr_ref, at_ref, wt_ref, yt_ref, *, nheads):
    tb, D = x_ref.shape
    te = u_ref.shape[0]
    nkeys = sk_ref.shape[1]
    half = sk_ref.shape[2]
    e_step = pl.program_id(1)

    @pl.when(e_step == 0)
    def _():
        x = x_ref[...]
        y = x * lax.rsqrt(jnp.mean(x * x, axis=-1, keepdims=True) + EPS)
        hn_ref[...] = (y * g_ref[...]).astype(BF16)
        qt_ref[...] = lax.dot_general(wq_ref[...], hn_ref[...], _NT, preferred_element_type=F32)
        yt_ref[...] = jnp.zeros(yt_ref.shape, F32)

        def head_body(h, carry):
            es = []
            for c in range(2):
                r0 = pl.multiple_of((h * 2 + c) * half, half)
                qs = qt_ref[pl.ds(r0, half), :].astype(BF16)
                s = jnp.dot(sk_ref[h * 2 + c], qs, preferred_element_type=F32)
                es.append(jnp.exp(s - jnp.max(s, axis=0, keepdims=True)))
            e1, e2 = es
            a = _top16_desc(e1)
            b = _top16_desc(e2)
            top = jnp.maximum(_top16_desc(_pair_candidates(a, b)), 0.0)
            rz = 1.0 / jnp.sum(top, axis=0, keepdims=True)
            e1n = e1 * rz
            thr = _top16_desc(_pair_candidates(a * rz, b))[PEER_TOPK - 1:PEER_TOPK, :]
            thr = jnp.maximum(thr, jnp.finfo(F32).tiny)
            e1_ref[h] = e1n
            e2_ref[h] = e2
            thr_ref[h] = jnp.broadcast_to(thr, (8, tb))
            return carry

        lax.fori_loop(0, nheads, head_body, 0)

    at_ref[...] = lax.dot_general(u_ref[...], hn_ref[...], _NT, preferred_element_type=F32)
    ni = te // nkeys
    i_base = e_step * ni

    def row8_body(i8, carry):
        i0 = pl.multiple_of(i_base + i8 * 8, 8)
        for tc in range(tb // LANES):
            sl = slice(tc * LANES, (tc + 1) * LANES)
            e1_tiles = [e1_ref[h, pl.ds(i0, 8), sl] for h in range(nheads)]
            for ii in range(8):
                gate = jnp.zeros((nkeys, LANES), F32)
                for h in range(nheads):
                    t = e2_ref[h, :, sl] * e1_tiles[h][ii:ii + 1, :]
                    gate = gate + jnp.where(t >= thr_ref[h, 0:1, sl], t, 0.0)
                r0 = pl.multiple_of((i8 * 8 + ii) * nkeys, nkeys)
                a = at_ref[pl.ds(r0, nkeys), sl]
                act = 0.5 * a * (1.0 + lax.erf(a * (1.0 / math.sqrt(2.0))))
                wt_ref[pl.ds(r0, nkeys), sl] = (gate * act).astype(BF16)
        return carry

    lax.fori_loop(0, ni // 8, row8_body, 0)
    yt_ref[...] += jnp.dot(vt_ref[...], wt_ref[...], preferred_element_type=F32)

    @pl.when(e_step == pl.num_programs(1) - 1)
    def _():
        o_ref[...] = x_ref[...] + yt_ref[...].T


def peer_layer(x, g, wq_t, sub_keys, u, v_t, tb, te):
    M, D = x.shape
    E = u.shape[0]
    nheads = sub_keys.shape[0] // 2
    nkeys = sub_keys.shape[1]
    return pl.pallas_call(
        functools.partial(_peer_kernel, nheads=nheads),
        grid=(M // tb, E // te),
        in_specs=[pl.BlockSpec((tb, D), lambda i, e: (i, 0)),
                  pl.BlockSpec((1, D), lambda i, e: (0, 0)),
                  pl.BlockSpec(wq_t.shape, lambda i, e: (0, 0)),
                  pl.BlockSpec(sub_keys.shape, lambda i, e: (0, 0, 0)),
                  pl.BlockSpec((te, D), lambda i, e: (e, 0)),
                  pl.BlockSpec((D, te), lambda i, e: (0, e))],
        out_specs=pl.BlockSpec((tb, D), lambda i, e: (i, 0)),
        out_shape=jax.ShapeDtypeStruct((M, D), F32),
        scratch_shapes=[pltpu.VMEM((tb, D), BF16),
                        pltpu.VMEM((wq_t.shape[0], tb), F32),
                        pltpu.VMEM((nheads, nkeys, tb), F32),
                        pltpu.VMEM((nheads, nkeys, tb), F32),
                        pltpu.VMEM((nheads, 8, tb), F32),
                        pltpu.VMEM((te, tb), F32),
                        pltpu.VMEM((te, tb), BF16),
                        pltpu.VMEM((D, tb), F32)],
        compiler_params=_params(("arbitrary", "arbitrary")),
        name="peer",
    )(x, g, wq_t, sub_keys, u, v_t)


def _rope_tables(pos):
    inv = ROPE_THETA ** (-jnp.arange(0, HEAD_DIM, 2, dtype=F32) / HEAD_DIM)
    ang = pos.astype(F32)[:, None] * inv[None, :]
    ang = jnp.concatenate([ang, ang, ang, ang], axis=-1)
    return jnp.cos(ang), jnp.sin(ang)


def _tile_heads(g, width):
    return jnp.tile(g, width // g.shape[0]).reshape(1, width)


def _forward(x_prompt, x_sample, cache_fox_k, cache_fox_v, cache_fox_logf, cache_diff_k,
             cache_diff_v, norm_mix_g, norm_ffn_g, fox_w_in, fox_b_f, fox_q_g, fox_k_g, fox_w_o,
             diff_w_in, diff_q_g, diff_k_g, diff_lambda, diff_sub_g, diff_w_o,
             peer_w_q, peer_sub_keys, peer_u, peer_v, *, tm, tq, tk_cache, tb, te):
    B, S, D = x_prompt.shape
    NS, T, _ = x_sample.shape
    depth = norm_mix_g.shape[0]
    past = cache_fox_k.shape[2]
    MP = B * S
    M = MP + NS * T
    fox_heads = cache_fox_k.shape[3]
    diff_heads = cache_diff_k.shape[3]
    W = fox_heads * HEAD_DIM
    npair = W // LANES

    x = jnp.concatenate([x_prompt.reshape(MP, D), x_sample.reshape(NS * T, D)], axis=0)

    pos = jnp.concatenate([jnp.tile(jnp.arange(S, dtype=jnp.int32), B),
                           jnp.tile(past + jnp.arange(T, dtype=jnp.int32), NS)])
    cos, sin = _rope_tables(pos)

    n_fox = cache_fox_k.shape[0]
    n_diff = cache_diff_k.shape[0]
    kc_fox = cache_fox_k.reshape(n_fox * NS, past, W)
    vc_fox = cache_fox_v.reshape(n_fox * NS, past, W)
    kc_diff = cache_diff_k.reshape(n_diff * NS, past, W)
    vc_diff = cache_diff_v.reshape(n_diff * NS, past, W)

    outs = {k: [] for k in ("fk_p", "fv_p", "fl_p", "dk_p", "dv_p",
                            "fk_s", "fv_s", "fl_s", "dk_s", "dv_s")}

    for layer in range(depth):
        j = layer // 2
        g_mix = norm_mix_g[layer].reshape(1, D)
        if layer % 2 == 0:
            w_in = fox_w_in[j]
            w_main = w_in[:, :4 * W].astype(BF16)
            w_f = jnp.pad(w_in[:, 4 * W:], ((0, 0), (0, LANES - fox_heads))).astype(BF16)
            p = norm_matmul(x, g_mix, w_main, tm, 1024)
            pf = norm_matmul(x, g_mix, w_f, tm, LANES)
            b_f = jnp.pad(fox_b_f[j], (0, LANES - fox_heads)).reshape(1, LANES)
            qn, kn, lf = fox_post(p, pf, b_f, _tile_heads(fox_q_g[j], W),
                                  _tile_heads(fox_k_g[j], W), tm)
            lf = lf[:, :fox_heads]
            lf_p = lf[:MP].reshape(B, S, fox_heads)
            lf_s = lf[MP:].reshape(NS, T, fox_heads)
            c_p = jnp.cumsum(lf_p, axis=1)
            c_all = jnp.cumsum(jnp.concatenate([cache_fox_logf[j], lf_s], axis=1), axis=1)
            c_pt = jnp.swapaxes(c_p, 1, 2).reshape(B, npair, 2, S)
            cq_p = jnp.swapaxes(c_pt, 2, 3)
            o_p = prompt_attention("fox", qn, kn, p, 2 * npair, B, S, tq,
                                   cq=cq_p, ck=c_pt, gate_src=p, gate_col0=3 * npair)
            c_at = jnp.swapaxes(c_all, 1, 2)
            ck_new = jnp.pad(c_at[:, :, past:], ((0, 0), (0, 0), (0, LANES - T)))
            cq_s = c_at[:, :, past:].reshape(NS, fox_heads * T, 1)
            o_s = sample_attention("fox", qn, kn, p, 2, kc_fox, vc_fox, j, NS, T, MP, tk_cache,
                                   cq=cq_s, ck_cache=c_at[:, :, :past], ck_new=ck_new,
                                   gate_src=p, gate_colblk=3)
            o = jnp.concatenate([o_p, o_s], axis=0)
            x = matmul_residual(o, fox_w_o[j].astype(BF16), x, tm)
            vv = p[:, 2 * W:3 * W]
            outs["fk_p"].append(kn[:MP].reshape(B, S, fox_heads, HEAD_DIM))
            outs["fv_p"].append(vv[:MP].reshape(B, S, fox_heads, HEAD_DIM))
            outs["fl_p"].append(lf_p)
            outs["fk_s"].append(kn[MP:].reshape(NS, T, fox_heads, HEAD_DIM))
            outs["fv_s"].append(vv[MP:].reshape(NS, T, fox_heads, HEAD_DIM))
            outs["fl_s"].append(lf_s)
        else:
            lam_init = 0.8 - 0.6 * math.exp(-0.3 * layer)
            p = norm_matmul(x, g_mix, diff_w_in[j].astype(BF16), tm, 1024)
            qn, kn = diff_post(p, cos, sin, _tile_heads(diff_q_g[j], W),
                               _tile_heads(diff_k_g[j], W), tm)
            sub_g = diff_sub_g[j].reshape(1, LANES)
            o_p = prompt_attention("diff", qn, kn, p, 2 * npair, B, S, tq,
                                   lam=diff_lambda[j], sub_g=sub_g, lam_init=lam_init)
            o_s = sample_attention("diff", qn, kn, p, 2, kc_diff, vc_diff, j, NS, T, MP, tk_cache,
                                   lam=diff_lambda[j], sub_g=sub_g, lam_init=lam_init)
            o = jnp.concatenate([o_p, o_s], axis=0)
            x = matmul_residual(o, diff_w_o[j].astype(BF16), x, tm)
            vv = p[:, 2 * W:3 * W]
            outs["dk_p"].append(kn[:MP].reshape(B, S, diff_heads, 2, HEAD_DIM))
            outs["dv_p"].append(vv[:MP].reshape(B, S, diff_heads, 2 * HEAD_DIM))
            outs["dk_s"].append(kn[MP:].reshape(NS, T, diff_heads, 2, HEAD_DIM))
            outs["dv_s"].append(vv[MP:].reshape(NS, T, diff_heads, 2 * HEAD_DIM))

        sk = peer_sub_keys[layer]
        x = peer_layer(x, norm_ffn_g[layer].reshape(1, D),
                       peer_w_q[layer].T.astype(BF16),
                       sk.reshape(sk.shape[0] * 2, sk.shape[2], sk.shape[3]).astype(BF16),
                       peer_u[layer].astype(BF16), peer_v[layer].T.astype(BF16), tb, te)

    return (x[:MP].reshape(B, S, D), x[MP:].reshape(NS, T, D),
            jnp.stack(outs["fk_p"]), jnp.stack(outs["fv_p"]), jnp.stack(outs["fl_p"]),
            jnp.stack(outs["dk_p"]), jnp.stack(outs["dv_p"]),
            jnp.stack(outs["fk_s"]), jnp.stack(outs["fv_s"]), jnp.stack(outs["fl_s"]),
            jnp.stack(outs["dk_s"]), jnp.stack(outs["dv_s"]))


def kernel(x_prompt, x_sample, cache_fox_k, cache_fox_v, cache_fox_logf, cache_diff_k, cache_diff_v, norm_mix_g, norm_ffn_g, fox_w_in, fox_b_f, fox_q_g, fox_k_g, fox_w_o, diff_w_in, diff_q_g, diff_k_g, diff_lambda, diff_sub_g, diff_w_o, peer_w_q, peer_sub_keys, peer_u, peer_v):
    return _forward(x_prompt, x_sample, cache_fox_k, cache_fox_v, cache_fox_logf, cache_diff_k,
                    cache_diff_v, norm_mix_g, norm_ffn_g, fox_w_in, fox_b_f, fox_q_g, fox_k_g,
                    fox_w_o, diff_w_in, diff_q_g, diff_k_g, diff_lambda, diff_sub_g, diff_w_o,
                    peer_w_q, peer_sub_keys, peer_u, peer_v,
                    tm=512, tq=512, tk_cache=512, tb=256, te=2048)
```

```python
import functools
import math

import jax
import jax.numpy as jnp
from jax import lax
from jax.experimental import pallas as pl
from jax.experimental.pallas import tpu as pltpu

F32 = jnp.float32
BF16 = jnp.bfloat16
EPS = 1e-6
NEG_INF = -1e30
CHUNK = 64
ROPE_THETA = 10000.0
LOG2E = math.log2(math.e)
PEER_TOPK = 16
LANES = 128
SUBLANES = 8
MXU_WIDTH = 256
HEAD_DIM = 64
VMEM_LIMIT = 56 * 1024 * 1024

_NT = (((1,), (1,)), ((), ()))


def _params(sem, vmem=VMEM_LIMIT):
    return pltpu.CompilerParams(dimension_semantics=sem, vmem_limit_bytes=vmem)


def _norm_matmul_kernel(x_ref, g_ref, w_ref, o_ref, hn_ref):
    @pl.when(pl.program_id(1) == 0)
    def _():
        x = x_ref[...]
        y = x * lax.rsqrt(jnp.mean(x * x, axis=-1, keepdims=True) + EPS)
        hn_ref[...] = (y * g_ref[...]).astype(BF16)

    o_ref[...] = jnp.dot(hn_ref[...], w_ref[...], preferred_element_type=F32)


def norm_matmul(x, g, w, tm, tn):
    M, D = x.shape
    N = w.shape[1]
    return pl.pallas_call(
        _norm_matmul_kernel,
        grid=(M // tm, N // tn),
        in_specs=[pl.BlockSpec((tm, D), lambda i, j: (i, 0)),
                  pl.BlockSpec((1, D), lambda i, j: (0, 0)),
                  pl.BlockSpec((D, tn), lambda i, j: (0, j))],
        out_specs=pl.BlockSpec((tm, tn), lambda i, j: (i, j)),
        out_shape=jax.ShapeDtypeStruct((M, N), F32),
        scratch_shapes=[pltpu.VMEM((tm, D), BF16)],
        compiler_params=_params(("arbitrary", "arbitrary")),
        name="norm_matmul",
    )(x, g, w)


def _matmul_res_kernel(a_ref, w_ref, r_ref, o_ref):
    o_ref[...] = r_ref[...] + jnp.dot(a_ref[...].astype(BF16), w_ref[...],
                                      preferred_element_type=F32)


def matmul_residual(a, w, res, tm):
    M, K = a.shape
    N = w.shape[1]
    return pl.pallas_call(
        _matmul_res_kernel,
        grid=(M // tm,),
        in_specs=[pl.BlockSpec((tm, K), lambda i: (i, 0)),
                  pl.BlockSpec((K, N), lambda i: (0, 0)),
                  pl.BlockSpec((tm, N), lambda i: (i, 0))],
        out_specs=pl.BlockSpec((tm, N), lambda i: (i, 0)),
        out_shape=jax.ShapeDtypeStruct((M, N), F32),
        compiler_params=_params(("arbitrary",)),
        name="matmul_residual",
    )(a, w, res)


def _head_rms64(xc, lo):
    sq = xc * xc
    s_lo = jnp.sum(jnp.where(lo, sq, 0.0), axis=-1, keepdims=True)
    s_hi = jnp.sum(jnp.where(lo, 0.0, sq), axis=-1, keepdims=True)
    ms = jnp.where(lo, s_lo, s_hi) * (1.0 / HEAD_DIM)
    return lax.rsqrt(ms + EPS)


def _fox_post_kernel(q_ref, k_ref, f_ref, bf_ref, qg_ref, kg_ref, qo_ref, ko_ref, lf_ref):
    tm, width = q_ref.shape
    lo = lax.broadcasted_iota(jnp.int32, (tm, LANES), 1) < HEAD_DIM
    for src, g_ref, dst in ((q_ref, qg_ref, qo_ref), (k_ref, kg_ref, ko_ref)):
        for c in range(width // LANES):
            sl = slice(c * LANES, (c + 1) * LANES)
            xc = src[:, sl]
            dst[:, sl] = xc * _head_rms64(xc, lo) * g_ref[:, sl]
    z = f_ref[...] + bf_ref[...]
    lf_ref[...] = jnp.minimum(z, 0.0) - jnp.log1p(jnp.exp(-jnp.abs(z)))


def fox_post(p, pf, b_f, q_g, k_g, tm):
    M = p.shape[0]
    W = q_g.shape[1]
    row = lambda c: pl.BlockSpec((tm, W), lambda i, c=c: (i, c))
    vec = lambda n: pl.BlockSpec((1, n), lambda i: (0, 0))
    return pl.pallas_call(
        _fox_post_kernel,
        grid=(M // tm,),
        in_specs=[row(0), row(1), pl.BlockSpec((tm, LANES), lambda i: (i, 0)),
                  vec(LANES), vec(W), vec(W)],
        out_specs=[row(0), row(0), pl.BlockSpec((tm, LANES), lambda i: (i, 0))],
        out_shape=[jax.ShapeDtypeStruct((M, W), F32), jax.ShapeDtypeStruct((M, W), F32),
                   jax.ShapeDtypeStruct((M, LANES), F32)],
        compiler_params=_params(("arbitrary",)),
        name="fox_post",
    )(p, p, pf, b_f, q_g, k_g)


def _diff_post_kernel(q_ref, k_ref, cos_ref, sin_ref, qg_ref, kg_ref, qo_ref, ko_ref):
    tm, width = q_ref.shape
    lane = lax.broadcasted_iota(jnp.int32, (tm, LANES), 1)
    lo = lane < HEAD_DIM
    first_half = (lane % HEAD_DIM) < (HEAD_DIM // 2)
    cos = cos_ref[...]
    sin = sin_ref[...]
    for src, g_ref, dst in ((q_ref, qg_ref, qo_ref), (k_ref, kg_ref, ko_ref)):
        for c in range(width // LANES):
            sl = slice(c * LANES, (c + 1) * LANES)
            xc = src[:, sl]
            xn = xc * _head_rms64(xc, lo) * g_ref[:, sl]
            rot = jnp.where(first_half,
                            -pltpu.roll(xn, LANES - HEAD_DIM // 2, axis=1),
                            pltpu.roll(xn, HEAD_DIM // 2, axis=1))
            dst[:, sl] = xn * cos + rot * sin


def diff_post(p, cos, sin, q_g, k_g, tm):
    M = p.shape[0]
    W = q_g.shape[1]
    row = lambda c: pl.BlockSpec((tm, W), lambda i, c=c: (i, c))
    tab = pl.BlockSpec((tm, LANES), lambda i: (i, 0))
    vec = pl.BlockSpec((1, W), lambda i: (0, 0))
    return pl.pallas_call(
        _diff_post_kernel,
        grid=(M // tm,),
        in_specs=[row(0), row(1), tab, tab, vec, vec],
        out_specs=[row(0), row(0)],
        out_shape=[jax.ShapeDtypeStruct((M, W), F32), jax.ShapeDtypeStruct((M, W), F32)],
        compiler_params=_params(("arbitrary",)),
        name="diff_post",
    )(p, p, cos, sin, q_g, k_g)


def _diff_lambda(lam_ref, lam_init):
    lf = lam_ref[...]
    a = jnp.sum(lf[0:1, :] * lf[1:2, :], axis=-1, keepdims=True)
    b = jnp.sum(lf[2:3, :] * lf[3:4, :], axis=-1, keepdims=True)
    return jnp.exp(a) - jnp.exp(b) + lam_init


def _prompt_attn_kernel(qi_tab, ki_tab, q_ref, k_ref, v_ref, *rest, mode, lam_init, scale):
    if mode == "fox":
        cq_ref, ck_ref, gate_ref, o_ref = rest[:4]
        scratch = rest[4:]
    else:
        lam_ref, subg_ref, o_ref = rest[:3]
        scratch = rest[3:]
    qs_ref, m0_ref, m1_ref, l0_ref, l1_ref, acc0_ref, acc1_ref = scratch
    tq = q_ref.shape[0]
    tk = k_ref.shape[0]
    t = pl.program_id(2)
    qi = qi_tab[t]
    ki = ki_tab[t]

    @pl.when(ki == 0)
    def _():
        qs_ref[...] = (q_ref[...] * scale).astype(BF16)
        for m_ref, l_ref, acc_ref in ((m0_ref, l0_ref, acc0_ref), (m1_ref, l1_ref, acc1_ref)):
            m_ref[...] = jnp.full(m_ref.shape, NEG_INF, F32)
            l_ref[...] = jnp.zeros(l_ref.shape, F32)
            acc_ref[...] = jnp.zeros(acc_ref.shape, F32)

    def update(s, vt, m_ref, l_ref, acc_ref):
        m_prev = m_ref[...]
        m_new = jnp.maximum(m_prev, jnp.max(s, axis=0, keepdims=True))
        alpha = jnp.exp2(m_prev - m_new)
        p = jnp.exp2(s - m_new)
        l_ref[...] = alpha * l_ref[...] + jnp.sum(p, axis=0, keepdims=True)
        acc_ref[...] = alpha * acc_ref[...] + jnp.dot(vt, p.astype(BF16),
                                                      preferred_element_type=F32)
        m_ref[...] = m_new

    def step(diagonal):
        k = k_ref[...]
        lo_k = lax.broadcasted_iota(jnp.int32, (tk, LANES), 1) < HEAD_DIM
        k0 = jnp.where(lo_k, k, 0.0).astype(BF16)
        k1 = jnp.where(lo_k, 0.0, k).astype(BF16)
        s0 = lax.dot_general(k0, qs_ref[...], _NT, preferred_element_type=F32)
        s1 = lax.dot_general(k1, qs_ref[...], _NT, preferred_element_type=F32)
        if mode == "fox":
            s0 = s0 + cq_ref[0:1, :] - ck_ref[:, 0:1]
            s1 = s1 + cq_ref[1:2, :] - ck_ref[:, 1:2]
        if diagonal:
            key = lax.broadcasted_iota(jnp.int32, (tk, tq), 0)
            qry = lax.broadcasted_iota(jnp.int32, (tk, tq), 1)
            if mode == "fox":
                visible = key <= qry
            else:
                visible = (key // CHUNK) <= (qry // CHUNK)
            s0 = jnp.where(visible, s0, NEG_INF)
            s1 = jnp.where(visible, s1, NEG_INF)
        vt = v_ref[...].T.astype(BF16)
        update(s0, vt, m0_ref, l0_ref, acc0_ref)
        update(s1, vt, m1_ref, l1_ref, acc1_ref)

    @pl.when(ki < qi)
    def _():
        step(False)

    @pl.when(ki == qi)
    def _():
        step(True)
        o0 = acc0_ref[...] / l0_ref[...]
        o1 = acc1_ref[...] / l1_ref[...]
        if mode == "fox":
            first_head = lax.broadcasted_iota(jnp.int32, (LANES, tq), 0) < HEAD_DIM
            ot = jnp.where(first_head, o0, o1)
            o_ref[...] = ot.T * jax.nn.sigmoid(gate_ref[...])
        else:
            ot = o0 - _diff_lambda(lam_ref, lam_init) * o1
            y = ot * lax.rsqrt(jnp.mean(ot * ot, axis=0, keepdims=True) + EPS)
            o_ref[...] = (y.T * subg_ref[...]) * (1.0 - lam_init)


def prompt_attention(mode, q, k, vsrc, v_col0, batch, seq, tq, *, c_rows=None, c_cols=None,
                     gate_src=None, gate_col0=None, lam=None, sub_g=None, lam_init=0.0):
    W = q.shape[1]
    npair = W // LANES
    nq = seq // tq
    tri = [(a, b) for a in range(nq) for b in range(a + 1)]
    qi_tab = jnp.asarray([a for a, _ in tri], jnp.int32)
    ki_tab = jnp.asarray([b for _, b in tri], jnp.int32)
    qmap = lambda b, h, t, qt, kt: (b * nq + qt[t], h)
    kmap = lambda b, h, t, qt, kt: (b * nq + kt[t], h)
    in_specs = [pl.BlockSpec((tq, LANES), qmap),
                pl.BlockSpec((tq, LANES), kmap),
                pl.BlockSpec((tq, LANES), lambda b, h, t, qt, kt: (b * nq + kt[t], v_col0 + h))]
    args = [q, k, vsrc]
    if mode == "fox":
        in_specs += [pl.BlockSpec((None, None, 2, tq), lambda b, h, t, qt, kt: (b, h, 0, qt[t])),
                     pl.BlockSpec((None, None, tq, 2), lambda b, h, t, qt, kt: (b, h, kt[t], 0)),
                     pl.BlockSpec((tq, LANES),
                                  lambda b, h, t, qt, kt: (b * nq + qt[t], gate_col0 + h))]
        args += [c_rows, c_cols, gate_src]
    else:
        in_specs += [pl.BlockSpec(lam.shape, lambda b, h, t, qt, kt: (0, 0)),
                     pl.BlockSpec((1, LANES), lambda b, h, t, qt, kt: (0, 0))]
        args += [lam, sub_g]
    stat = pltpu.VMEM((1, tq), F32)
    acc = pltpu.VMEM((LANES, tq), F32)
    grid_spec = pltpu.PrefetchScalarGridSpec(
        num_scalar_prefetch=2,
        grid=(batch, npair, len(tri)),
        in_specs=in_specs,
        out_specs=pl.BlockSpec((tq, LANES), qmap),
        scratch_shapes=[pltpu.VMEM((tq, LANES), BF16), stat, stat, stat, stat, acc, acc],
    )
    return pl.pallas_call(
        functools.partial(_prompt_attn_kernel, mode=mode, lam_init=lam_init,
                          scale=LOG2E / math.sqrt(HEAD_DIM)),
        grid_spec=grid_spec,
        out_shape=jax.ShapeDtypeStruct((batch * seq, W), F32),
        compiler_params=_params(("arbitrary", "arbitrary", "arbitrary")),
        name=f"prompt_attention_{mode}",
    )(qi_tab, ki_tab, *args)


def _sample_attn_kernel(q_ref, kc_ref, vc_ref, kn_ref, vn_ref, *rest, mode, lam_init, scale,
                        past, v_time_minor):
    if mode == "fox":
        cq_ref, ckc_ref, ckn_ref, gate_ref, o_ref = rest[:5]
        scratch = rest[5:]
    else:
        lam_ref, subg_ref, o_ref = rest[:3]
        scratch = rest[3:]
    qbd_ref, m_ref, l_ref, acc_ref = scratch
    tnew, W = q_ref.shape
    nmap = W // HEAD_DIM
    R = nmap * tnew
    kstep = pl.program_id(1)
    nk = pl.num_programs(1) - 1

    @pl.when(kstep == 0)
    def _():
        q = q_ref[...] * scale
        qt = jnp.concatenate([q] * nmap, axis=0)
        rmap = lax.broadcasted_iota(jnp.int32, (R, W), 0) // tnew
        cmap = lax.broadcasted_iota(jnp.int32, (R, W), 1) // HEAD_DIM
        qbd_ref[...] = jnp.where(rmap == cmap, qt, 0.0).astype(BF16)
        m_ref[...] = jnp.full(m_ref.shape, NEG_INF, F32)
        l_ref[...] = jnp.zeros(l_ref.shape, F32)
        acc_ref[...] = jnp.zeros(acc_ref.shape, F32)

    def expand_rows(c):
        n = c.shape[1]
        return jnp.broadcast_to(c[:, None, :], (nmap, tnew, n)).reshape(R, n)

    def update(s, pv):
        m_prev = m_ref[...]
        m_new = jnp.maximum(m_prev, jnp.max(s, axis=-1, keepdims=True))
        alpha = jnp.exp(m_prev - m_new)
        p = jnp.exp(s - m_new)
        l_ref[...] = alpha * l_ref[...] + jnp.sum(p, axis=-1, keepdims=True)
        acc_ref[...] = alpha * acc_ref[...] + pv(p.astype(BF16))
        m_ref[...] = m_new

    @pl.when(kstep < nk)
    def _():
        s = jnp.dot(qbd_ref[...], kc_ref[...].astype(BF16), preferred_element_type=F32)
        if mode == "fox":
            s = s + cq_ref[...] - expand_rows(ckc_ref[...])
        v = vc_ref[...].astype(BF16)
        if v_time_minor:
            update(s, lambda p: lax.dot_general(p, v, _NT, preferred_element_type=F32))
        else:
            update(s, lambda p: jnp.dot(p, v, preferred_element_type=F32))

    @pl.when(kstep == nk)
    def _():
        npad = LANES - tnew
        k = jnp.concatenate([kn_ref[...], jnp.zeros((npad, W), F32)], axis=0).astype(BF16)
        v = jnp.concatenate([vn_ref[...], jnp.zeros((npad, W), F32)], axis=0).astype(BF16)
        s = lax.dot_general(qbd_ref[...], k, _NT, preferred_element_type=F32)
        if mode == "fox":
            s = s + cq_ref[...] - expand_rows(ckn_ref[...])
        tpos = lax.broadcasted_iota(jnp.int32, (R, LANES), 0) % tnew
        spos = lax.broadcasted_iota(jnp.int32, (R, LANES), 1)
        if mode == "fox":
            visible = spos <= tpos
        else:
            visible = ((past + spos) // CHUNK) <= ((past + tpos) // CHUNK)
        visible = jnp.logical_and(visible, spos < tnew)
        update(jnp.where(visible, s, NEG_INF),
               lambda p: jnp.dot(p, v, preferred_element_type=F32))

        accn = acc_ref[...] / l_ref[...]
        if mode == "fox":
            lane_map = lax.broadcasted_iota(jnp.int32, (tnew, W), 1) // HEAD_DIM
            o = jnp.zeros((tnew, W), F32)
            for r in range(nmap):
                o = jnp.where(lane_map == r, accn[r * tnew:(r + 1) * tnew, :], o)
            o_ref[...] = o * jax.nn.sigmoid(gate_ref[...])
        else:
            lam = _diff_lambda(lam_ref, lam_init)
            for h in range(nmap // 2):
                sl = slice(h * LANES, (h + 1) * LANES)
                o = (accn[(2 * h) * tnew:(2 * h + 1) * tnew, sl]
                     - lam * accn[(2 * h + 1) * tnew:(2 * h + 2) * tnew, sl])
                y = o * lax.rsqrt(jnp.mean(o * o, axis=-1, keepdims=True) + EPS)
                o_ref[:, sl] = (y * subg_ref[...]) * (1.0 - lam_init)


def sample_attention(mode, q, k, vsrc, v_colblk, kcache_t, vcache, v_time_minor, layer_idx,
                     nstream, tnew, row0, tk, *, cq=None, ck_cache=None, ck_new=None,
                     gate_src=None, gate_colblk=None, lam=None, sub_g=None, lam_init=0.0):
    W = q.shape[1]
    past = kcache_t.shape[2]
    nk = past // tk
    nmap = W // HEAD_DIM
    R = nmap * tnew
    rb0 = row0 // tnew
    wblk = lambda colblk: pl.BlockSpec((tnew, W), lambda b, s, c=colblk: (rb0 + b, c))
    time_minor = pl.BlockSpec((None, W, tk),
                              lambda b, s: (layer_idx * nstream + b, 0, jnp.minimum(s, nk - 1)))
    time_major = pl.BlockSpec((None, tk, W),
                              lambda b, s: (layer_idx * nstream + b, jnp.minimum(s, nk - 1), 0))
    in_specs = [wblk(0), time_minor, time_minor if v_time_minor else time_major,
                wblk(0), wblk(v_colblk)]
    args = [q, kcache_t, vcache, k, vsrc]
    if mode == "fox":
        in_specs += [pl.BlockSpec((None, R, 1), lambda b, s: (b, 0, 0)),
                     pl.BlockSpec((None, nmap, tk), lambda b, s: (b, 0, jnp.minimum(s, nk - 1))),
                     pl.BlockSpec((None, nmap, LANES), lambda b, s: (b, 0, 0)),
                     wblk(gate_colblk)]
        args += [cq, ck_cache, ck_new, gate_src]
    else:
        in_specs += [pl.BlockSpec(lam.shape, lambda b, s: (0, 0)),
                     pl.BlockSpec((1, LANES), lambda b, s: (0, 0))]
        args += [lam, sub_g]
    return pl.pallas_call(
        functools.partial(_sample_attn_kernel, mode=mode, lam_init=lam_init,
                          scale=1.0 / math.sqrt(HEAD_DIM), past=past, v_time_minor=v_time_minor),
        grid=(nstream, nk + 1),
        in_specs=in_specs,
        out_specs=pl.BlockSpec((tnew, W), lambda b, s: (b, 0)),
        out_shape=jax.ShapeDtypeStruct((nstream * tnew, W), F32),
        scratch_shapes=[pltpu.VMEM((R, W), BF16), pltpu.VMEM((R, 1), F32),
                        pltpu.VMEM((R, 1), F32), pltpu.VMEM((R, W), F32)],
        compiler_params=_params(("arbitrary", "arbitrary")),
        name=f"sample_attention_{mode}",
    )(*args)


def _top16_desc(e):
    rows = []
    for _ in range(PEER_TOPK):
        m = jnp.max(e, axis=0, keepdims=True)
        rows.append(m)
        e = jnp.where(e == m, -1.0, e)
    return jnp.concatenate(rows, axis=0)


def _pair_candidates(a, b):
    tb = a.shape[1]
    row = lax.broadcasted_iota(jnp.int32, (SUBLANES, tb), 0)
    b8 = b[0:8, :]
    parts = [b * a[0:1, :], b8 * a[1:2, :]]
    for r1 in range(2, 8):
        parts.append(jnp.where(row < PEER_TOPK // (r1 + 1), b8 * a[r1:r1 + 1, :], -1.0))
    parts.append(b[0:1, :] * a[8:16, :])
    return jnp.concatenate(parts, axis=0)


def _peer_kernel(x_ref, g_ref, wq_ref, sk_ref, u0_ref, un_ref, vp_ref, vl_ref, o_ref,
                 hn_ref, e1_ref, e2_ref, thr_ref, at0_ref, at1_ref, wt0_ref, wt1_ref, yt_ref,
                 *, nheads):
    tb, D = x_ref.shape
    te = un_ref.shape[0]
    nkeys = sk_ref.shape[1]
    half = sk_ref.shape[2]
    e_step = pl.program_id(1)
    ne = pl.num_programs(1)
    assert te == SUBLANES * nkeys
    chunk_w = min(MXU_WIDTH, tb)
    nchunk = tb // chunk_w

    @pl.when(e_step == 0)
    def _():
        x = x_ref[...]
        y = x * lax.rsqrt(jnp.mean(x * x, axis=-1, keepdims=True) + EPS)
        hn_ref[...] = (y * g_ref[...]).astype(BF16)

        def head_body(h, carry):
            es = []
            for c in range(2):
                r0 = pl.multiple_of((h * 2 + c) * half, half)
                qs = lax.dot_general(wq_ref[pl.ds(r0, half), :], hn_ref[...], _NT,
                                     preferred_element_type=F32).astype(BF16)
                s = jnp.dot(sk_ref[h * 2 + c], qs, preferred_element_type=F32)
                es.append(jnp.exp(s - jnp.max(s, axis=0, keepdims=True)))
            e1, e2 = es
            a = _top16_desc(e1)
            b = _top16_desc(e2)
            top = jnp.maximum(_top16_desc(_pair_candidates(a, b)), 0.0)
            rz = 1.0 / jnp.sum(top, axis=0, keepdims=True)
            e1n = e1 * rz
            thr = _top16_desc(_pair_candidates(a * rz, b))[PEER_TOPK - 1:PEER_TOPK, :]
            thr = jnp.maximum(thr, jnp.finfo(F32).tiny)
            e1_ref[h] = e1n
            e2_ref[h] = e2
            thr_ref[h] = jnp.broadcast_to(thr, (SUBLANES, tb))
            return carry

        lax.fori_loop(0, nheads, head_body, 0)
        at0_ref[...] = lax.dot_general(u0_ref[...], hn_ref[...], _NT, preferred_element_type=F32)
        wt1_ref[...] = jnp.zeros(wt1_ref.shape, BF16)
        yt_ref[...] = jnp.zeros(yt_ref.shape, F32)

    def mxu_chunk(c, at_nxt, wt_prv):
        cols = slice((c % nchunk) * chunk_w, (c % nchunk + 1) * chunk_w)
        if c < nchunk:
            at_nxt[:, cols] = lax.dot_general(un_ref[...], hn_ref[cols, :], _NT,
                                              preferred_element_type=F32)
        else:
            yt_ref[:, cols] += jnp.dot(vp_ref[...], wt_prv[:, cols],
                                       preferred_element_type=F32)

    def body(at_cur, at_nxt, wt_cur, wt_prv):
        i0 = pl.multiple_of(e_step * SUBLANES, SUBLANES)
        n_gate = tb // LANES
        n_mxu = 2 * nchunk
        for tc in range(n_gate):
            for c in range(tc * n_mxu // n_gate, (tc + 1) * n_mxu // n_gate):
                mxu_chunk(c, at_nxt, wt_prv)
            sl = slice(tc * LANES, (tc + 1) * LANES)
            e1_tiles = [e1_ref[h, pl.ds(i0, SUBLANES), sl] for h in range(nheads)]
            for ii in range(SUBLANES):
                gate = jnp.zeros((nkeys, LANES), F32)
                for h in range(nheads):
                    t = e2_ref[h, :, sl] * e1_tiles[h][ii:ii + 1, :]
                    gate = gate + jnp.where(t >= thr_ref[h, 0:1, sl], t, 0.0)
                rows = slice(ii * nkeys, (ii + 1) * nkeys)
                a = at_cur[rows, sl]
                act = 0.5 * a * (1.0 + lax.erf(a * (1.0 / math.sqrt(2.0))))
                wt_cur[rows, sl] = (gate * act).astype(BF16)

    @pl.when(e_step % 2 == 0)
    def _():
        body(at0_ref, at1_ref, wt0_ref, wt1_ref)

    @pl.when(e_step % 2 == 1)
    def _():
        body(at1_ref, at0_ref, wt1_ref, wt0_ref)

    @pl.when(e_step == ne - 1)
    def _():
        yt = yt_ref[...] + jnp.dot(vl_ref[...], wt1_ref[...], preferred_element_type=F32)
        o_ref[...] = x_ref[...] + yt.T


def peer_layer(x, g, wq_t, sub_keys, u, v_t, tb):
    M, D = x.shape
    E = u.shape[0]
    nheads = sub_keys.shape[0] // 2
    nkeys = sub_keys.shape[1]
    te = SUBLANES * nkeys
    ne = E // te
    assert ne % 2 == 0
    return pl.pallas_call(
        functools.partial(_peer_kernel, nheads=nheads),
        grid=(M // tb, ne),
        in_specs=[pl.BlockSpec((tb, D), lambda i, e: (i, 0)),
                  pl.BlockSpec((1, D), lambda i, e: (0, 0)),
                  pl.BlockSpec(wq_t.shape, lambda i, e: (0, 0)),
                  pl.BlockSpec(sub_keys.shape, lambda i, e: (0, 0, 0)),
                  pl.BlockSpec((te, D), lambda i, e: (0, 0)),
                  pl.BlockSpec((te, D), lambda i, e: (jnp.minimum(e + 1, ne - 1), 0)),
                  pl.BlockSpec((D, te), lambda i, e: (0, jnp.maximum(e - 1, 0))),
                  pl.BlockSpec((D, te), lambda i, e: (0, ne - 1))],
        out_specs=pl.BlockSpec((tb, D), lambda i, e: (i, 0)),
        out_shape=jax.ShapeDtypeStruct((M, D), F32),
        scratch_shapes=[pltpu.VMEM((tb, D), BF16),
                        pltpu.VMEM((nheads, nkeys, tb), F32),
                        pltpu.VMEM((nheads, nkeys, tb), F32),
                        pltpu.VMEM((nheads, SUBLANES, tb), F32),
                        pltpu.VMEM((te, tb), F32),
                        pltpu.VMEM((te, tb), F32),
                        pltpu.VMEM((te, tb), BF16),
                        pltpu.VMEM((te, tb), BF16),
                        pltpu.VMEM((D, tb), F32)],
        compiler_params=_params(("arbitrary", "arbitrary")),
        name="peer",
    )(x, g, wq_t, sub_keys, u, u, v_t, v_t)


def _rope_tables(pos):
    inv = ROPE_THETA ** (-jnp.arange(0, HEAD_DIM, 2, dtype=F32) / HEAD_DIM)
    ang = pos.astype(F32)[:, None] * inv[None, :]
    ang = jnp.concatenate([ang, ang, ang, ang], axis=-1)
    return jnp.cos(ang), jnp.sin(ang)


def _tile_heads(g, width):
    return jnp.tile(g, width // g.shape[0]).reshape(1, width)


def _time_minor(cache):
    nd = cache.ndim
    c = jnp.transpose(cache, (0, 1) + tuple(range(3, nd)) + (2,))
    return c.reshape(c.shape[0] * c.shape[1], -1, c.shape[-1])


def _forward(x_prompt, x_sample, cache_fox_k, cache_fox_v, cache_fox_logf, cache_diff_k,
             cache_diff_v, norm_mix_g, norm_ffn_g, fox_w_in, fox_b_f, fox_q_g, fox_k_g, fox_w_o,
             diff_w_in, diff_q_g, diff_k_g, diff_lambda, diff_sub_g, diff_w_o,
             peer_w_q, peer_sub_keys, peer_u, peer_v, *, tm, tq, tk_cache, tb):
    B, S, D = x_prompt.shape
    NS, T, _ = x_sample.shape
    depth = norm_mix_g.shape[0]
    past = cache_fox_k.shape[2]
    MP = B * S
    M = MP + NS * T
    fox_heads = cache_fox_k.shape[3]
    diff_heads = cache_diff_k.shape[3]
    W = fox_heads * HEAD_DIM
    npair = W // LANES

    x = jnp.concatenate([x_prompt.reshape(MP, D), x_sample.reshape(NS * T, D)], axis=0)

    pos = jnp.concatenate([jnp.tile(jnp.arange(S, dtype=jnp.int32), B),
                           jnp.tile(past + jnp.arange(T, dtype=jnp.int32), NS)])
    cos, sin = _rope_tables(pos)

    kc_fox = _time_minor(cache_fox_k)
    vc_fox = _time_minor(cache_fox_v)
    kc_diff = _time_minor(cache_diff_k)
    vc_diff = cache_diff_v.reshape(-1, past, W)
    lf_cache = jnp.swapaxes(cache_fox_logf, 2, 3)

    outs = {k: [] for k in ("fk_p", "fv_p", "fl_p", "dk_p", "dv_p",
                            "fk_s", "fv_s", "fl_s", "dk_s", "dv_s")}

    for layer in range(depth):
        j = layer // 2
        g_mix = norm_mix_g[layer].reshape(1, D)
        if layer % 2 == 0:
            w_in = fox_w_in[j]
            w_main = w_in[:, :4 * W].astype(BF16)
            w_f = jnp.pad(w_in[:, 4 * W:], ((0, 0), (0, LANES - fox_heads))).astype(BF16)
            p = norm_matmul(x, g_mix, w_main, tm, 1024)
            pf = norm_matmul(x, g_mix, w_f, tm, LANES)
            b_f = jnp.pad(fox_b_f[j], (0, LANES - fox_heads)).reshape(1, LANES)
            qn, kn, lf = fox_post(p, pf, b_f, _tile_heads(fox_q_g[j], W),
                                  _tile_heads(fox_k_g[j], W), tm)
            lf = lf[:, :fox_heads]
            lf_p = lf[:MP].reshape(B, S, fox_heads)
            lf_s = lf[MP:].reshape(NS, T, fox_heads)
            c_pt = jnp.cumsum(jnp.swapaxes(lf_p, 1, 2), axis=2)
            c_rows = (c_pt * LOG2E).reshape(B, npair, 2, S)
            c_cols = jnp.swapaxes(c_rows, 2, 3)
            o_p = prompt_attention("fox", qn, kn, p, 2 * npair, B, S, tq,
                                   c_rows=c_rows, c_cols=c_cols, gate_src=p,
                                   gate_col0=3 * npair)
            c_at = jnp.cumsum(jnp.concatenate([lf_cache[j], jnp.swapaxes(lf_s, 1, 2)], axis=2),
                              axis=2)
            ck_new = jnp.pad(c_at[:, :, past:], ((0, 0), (0, 0), (0, LANES - T)))
            cq_s = c_at[:, :, past:].reshape(NS, fox_heads * T, 1)
            o_s = sample_attention("fox", qn, kn, p, 2, kc_fox, vc_fox, True, j, NS, T, MP,
                                   tk_cache, cq=cq_s, ck_cache=c_at[:, :, :past], ck_new=ck_new,
                                   gate_src=p, gate_colblk=3)
            o = jnp.concatenate([o_p, o_s], axis=0)
            x = matmul_residual(o, fox_w_o[j].astype(BF16), x, tm)
            vv = p[:, 2 * W:3 * W]
            outs["fk_p"].append(kn[:MP].reshape(B, S, fox_heads, HEAD_DIM))
            outs["fv_p"].append(vv[:MP].reshape(B, S, fox_heads, HEAD_DIM))
            outs["fl_p"].append(lf_p)
            outs["fk_s"].append(kn[MP:].reshape(NS, T, fox_heads, HEAD_DIM))
            outs["fv_s"].append(vv[MP:].reshape(NS, T, fox_heads, HEAD_DIM))
            outs["fl_s"].append(lf_s)
        else:
            lam_init = 0.8 - 0.6 * math.exp(-0.3 * layer)
            p = norm_matmul(x, g_mix, diff_w_in[j].astype(BF16), tm, 1024)
            qn, kn = diff_post(p, cos, sin, _tile_heads(diff_q_g[j], W),
                               _tile_heads(diff_k_g[j], W), tm)
            sub_g = diff_sub_g[j].reshape(1, LANES)
            o_p = prompt_attention("diff", qn, kn, p, 2 * npair, B, S, tq,
                                   lam=diff_lambda[j], sub_g=sub_g, lam_init=lam_init)
            o_s = sample_attention("diff", qn, kn, p, 2, kc_diff, vc_diff, False, j, NS, T, MP,
                                   tk_cache, lam=diff_lambda[j], sub_g=sub_g, lam_init=lam_init)
            o = jnp.concatenate([o_p, o_s], axis=0)
            x = matmul_residual(o, diff_w_o[j].astype(BF16), x, tm)
            vv = p[:, 2 * W:3 * W]
            outs["dk_p"].append(kn[:MP].reshape(B, S, diff_heads, 2, HEAD_DIM))
            outs["dv_p"].append(vv[:MP].reshape(B, S, diff_heads, 2 * HEAD_DIM))
            outs["dk_s"].append(kn[MP:].reshape(NS, T, diff_heads, 2, HEAD_DIM))
            outs["dv_s"].append(vv[MP:].reshape(NS, T, diff_heads, 2 * HEAD_DIM))

        sk = peer_sub_keys[layer]
        x = peer_layer(x, norm_ffn_g[layer].reshape(1, D),
                       peer_w_q[layer].T.astype(BF16),
                       sk.reshape(sk.shape[0] * 2, sk.shape[2], sk.shape[3]).astype(BF16),
                       peer_u[layer].astype(BF16), peer_v[layer].T.astype(BF16), tb)

    return (x[:MP].reshape(B, S, D), x[MP:].reshape(NS, T, D),
            jnp.stack(outs["fk_p"]), jnp.stack(outs["fv_p"]), jnp.stack(outs["fl_p"]),
            jnp.stack(outs["dk_p"]), jnp.stack(outs["dv_p"]),
            jnp.stack(outs["fk_s"]), jnp.stack(outs["fv_s"]), jnp.stack(outs["fl_s"]),
            jnp.stack(outs["dk_s"]), jnp.stack(outs["dv_s"]))


def kernel(x_prompt, x_sample, cache_fox_k, cache_fox_v, cache_fox_logf, cache_diff_k, cache_diff_v, norm_mix_g, norm_ffn_g, fox_w_in, fox_b_f, fox_q_g, fox_k_g, fox_w_o, diff_w_in, diff_q_g, diff_k_g, diff_lambda, diff_sub_g, diff_w_o, peer_w_q, peer_sub_keys, peer_u, peer_v):
    return _forward(x_prompt, x_sample, cache_fox_k, cache_fox_v, cache_fox_logf, cache_diff_k,
                    cache_diff_v, norm_mix_g, norm_ffn_g, fox_w_in, fox_b_f, fox_q_g, fox_k_g,
                    fox_w_o, diff_w_in, diff_q_g, diff_k_g, diff_lambda, diff_sub_g, diff_w_o,
                    peer_w_q, peer_sub_keys, peer_u, peer_v,
                    tm=512, tq=512, tk_cache=512, tb=512)
```

```python
import functools
import math

import jax
import jax.numpy as jnp
from jax import lax
from jax.experimental import pallas as pl
from jax.experimental.pallas import tpu as pltpu

F32 = jnp.float32
BF16 = jnp.bfloat16
EPS = 1e-6
NEG_INF = -1e30
CHUNK = 64
ROPE_THETA = 10000.0
LOG2E = math.log2(math.e)
PEER_TOPK = 16
LANES = 128
SUBLANES = 8
MXU_WIDTH = 256
GATE_ROWS = 32
HEAD_DIM = 64
VMEM_LIMIT = 56 * 1024 * 1024

_NT = (((1,), (1,)), ((), ()))


def _params(sem, vmem=VMEM_LIMIT):
    return pltpu.CompilerParams(dimension_semantics=sem, vmem_limit_bytes=vmem)


def _norm_matmul_kernel(x_ref, g_ref, w_ref, o_ref, hn_ref):
    @pl.when(pl.program_id(1) == 0)
    def _():
        x = x_ref[...]
        y = x * lax.rsqrt(jnp.mean(x * x, axis=-1, keepdims=True) + EPS)
        hn_ref[...] = (y * g_ref[...]).astype(BF16)

    o_ref[...] = jnp.dot(hn_ref[...], w_ref[...], preferred_element_type=F32)


def norm_matmul(x, g, w, tm, tn):
    M, D = x.shape
    N = w.shape[1]
    return pl.pallas_call(
        _norm_matmul_kernel,
        grid=(M // tm, N // tn),
        in_specs=[pl.BlockSpec((tm, D), lambda i, j: (i, 0)),
                  pl.BlockSpec((1, D), lambda i, j: (0, 0)),
                  pl.BlockSpec((D, tn), lambda i, j: (0, j))],
        out_specs=pl.BlockSpec((tm, tn), lambda i, j: (i, j)),
        out_shape=jax.ShapeDtypeStruct((M, N), F32),
        scratch_shapes=[pltpu.VMEM((tm, D), BF16)],
        compiler_params=_params(("arbitrary", "arbitrary")),
        name="norm_matmul",
    )(x, g, w)


def _matmul_res_kernel(a_ref, w_ref, r_ref, o_ref):
    o_ref[...] = r_ref[...] + jnp.dot(a_ref[...].astype(BF16), w_ref[...],
                                      preferred_element_type=F32)


def matmul_residual(a, w, res, tm):
    M, K = a.shape
    N = w.shape[1]
    return pl.pallas_call(
        _matmul_res_kernel,
        grid=(M // tm,),
        in_specs=[pl.BlockSpec((tm, K), lambda i: (i, 0)),
                  pl.BlockSpec((K, N), lambda i: (0, 0)),
                  pl.BlockSpec((tm, N), lambda i: (i, 0))],
        out_specs=pl.BlockSpec((tm, N), lambda i: (i, 0)),
        out_shape=jax.ShapeDtypeStruct((M, N), F32),
        compiler_params=_params(("arbitrary",)),
        name="matmul_residual",
    )(a, w, res)


def _head_rms64(xc, lo):
    sq = xc * xc
    s_lo = jnp.sum(jnp.where(lo, sq, 0.0), axis=-1, keepdims=True)
    s_hi = jnp.sum(jnp.where(lo, 0.0, sq), axis=-1, keepdims=True)
    ms = jnp.where(lo, s_lo, s_hi) * (1.0 / HEAD_DIM)
    return lax.rsqrt(ms + EPS)


def _fox_post_kernel(q_ref, k_ref, f_ref, bf_ref, qg_ref, kg_ref, qo_ref, ko_ref, lf_ref,
                     lft_ref):
    tm, width = q_ref.shape
    lo = lax.broadcasted_iota(jnp.int32, (tm, LANES), 1) < HEAD_DIM
    for src, g_ref, dst in ((q_ref, qg_ref, qo_ref), (k_ref, kg_ref, ko_ref)):
        for c in range(width // LANES):
            sl = slice(c * LANES, (c + 1) * LANES)
            xc = src[:, sl]
            dst[:, sl] = xc * _head_rms64(xc, lo) * g_ref[:, sl]
    z = f_ref[...] + bf_ref[...]
    lf = jnp.minimum(z, 0.0) - jnp.log1p(jnp.exp(-jnp.abs(z)))
    lf_ref[...] = lf
    lft_ref[...] = lf.T


def fox_post(p, pf, b_f, q_g, k_g, tm):
    M = p.shape[0]
    W = q_g.shape[1]
    row = lambda c: pl.BlockSpec((tm, W), lambda i, c=c: (i, c))
    vec = lambda n: pl.BlockSpec((1, n), lambda i: (0, 0))
    return pl.pallas_call(
        _fox_post_kernel,
        grid=(M // tm,),
        in_specs=[row(0), row(1), pl.BlockSpec((tm, LANES), lambda i: (i, 0)),
                  vec(LANES), vec(W), vec(W)],
        out_specs=[row(0), row(0), pl.BlockSpec((tm, LANES), lambda i: (i, 0)),
                   pl.BlockSpec((LANES, tm), lambda i: (0, i))],
        out_shape=[jax.ShapeDtypeStruct((M, W), F32), jax.ShapeDtypeStruct((M, W), F32),
                   jax.ShapeDtypeStruct((M, LANES), F32), jax.ShapeDtypeStruct((LANES, M), F32)],
        compiler_params=_params(("arbitrary",)),
        name="fox_post",
    )(p, p, pf, b_f, q_g, k_g)


def _diff_post_kernel(q_ref, k_ref, cos_ref, sin_ref, qg_ref, kg_ref, qo_ref, ko_ref):
    tm, width = q_ref.shape
    lane = lax.broadcasted_iota(jnp.int32, (tm, LANES), 1)
    lo = lane < HEAD_DIM
    first_half = (lane % HEAD_DIM) < (HEAD_DIM // 2)
    cos = cos_ref[...]
    sin = sin_ref[...]
    for src, g_ref, dst in ((q_ref, qg_ref, qo_ref), (k_ref, kg_ref, ko_ref)):
        for c in range(width // LANES):
            sl = slice(c * LANES, (c + 1) * LANES)
            xc = src[:, sl]
            xn = xc * _head_rms64(xc, lo) * g_ref[:, sl]
            rot = jnp.where(first_half,
                            -pltpu.roll(xn, LANES - HEAD_DIM // 2, axis=1),
                            pltpu.roll(xn, HEAD_DIM // 2, axis=1))
            dst[:, sl] = xn * cos + rot * sin


def diff_post(p, cos, sin, q_g, k_g, tm):
    M = p.shape[0]
    W = q_g.shape[1]
    row = lambda c: pl.BlockSpec((tm, W), lambda i, c=c: (i, c))
    tab = pl.BlockSpec((tm, LANES), lambda i: (i, 0))
    vec = pl.BlockSpec((1, W), lambda i: (0, 0))
    return pl.pallas_call(
        _diff_post_kernel,
        grid=(M // tm,),
        in_specs=[row(0), row(1), tab, tab, vec, vec],
        out_specs=[row(0), row(0)],
        out_shape=[jax.ShapeDtypeStruct((M, W), F32), jax.ShapeDtypeStruct((M, W), F32)],
        compiler_params=_params(("arbitrary",)),
        name="diff_post",
    )(p, p, cos, sin, q_g, k_g)


def _cumsum_kernel(x_ref, o_ref):
    rows, length = x_ref.shape
    s_idx = lax.broadcasted_iota(jnp.int32, (LANES, LANES), 0)
    t_idx = lax.broadcasted_iota(jnp.int32, (LANES, LANES), 1)
    tri = (s_idx <= t_idx).astype(F32)
    carry = jnp.zeros((rows, 1), F32)
    for blk in range(length // LANES):
        sl = slice(blk * LANES, (blk + 1) * LANES)
        c = jnp.dot(x_ref[:, sl], tri, preferred_element_type=F32,
                    precision=lax.Precision.HIGHEST) + carry
        o_ref[:, sl] = c
        carry = c[:, LANES - 1:LANES]


def cumsum_lanes(x, rows, length):
    R, L = x.shape
    return pl.pallas_call(
        _cumsum_kernel,
        grid=(R // rows, L // length),
        in_specs=[pl.BlockSpec((rows, length), lambda i, j: (i, j))],
        out_specs=pl.BlockSpec((rows, length), lambda i, j: (i, j)),
        out_shape=jax.ShapeDtypeStruct((R, L), F32),
        compiler_params=_params(("arbitrary", "arbitrary")),
        name="cumsum_lanes",
    )(x)


def _diff_lambda(lam_ref, lam_init):
    lf = lam_ref[...]
    a = jnp.sum(lf[0:1, :] * lf[1:2, :], axis=-1, keepdims=True)
    b = jnp.sum(lf[2:3, :] * lf[3:4, :], axis=-1, keepdims=True)
    return jnp.exp(a) - jnp.exp(b) + lam_init


def _prompt_attn_kernel(qi_tab, ki_tab, q_ref, k_ref, v_ref, *rest, mode, lam_init, scale):
    if mode == "fox":
        cq_ref, ck_ref, gate_ref, o_ref = rest[:4]
        scratch = rest[4:]
    else:
        lam_ref, subg_ref, o_ref = rest[:3]
        scratch = rest[3:]
    qs_ref, m0_ref, m1_ref, l0_ref, l1_ref, acc0_ref, acc1_ref = scratch
    tq = q_ref.shape[0]
    tk = k_ref.shape[0]
    t = pl.program_id(2)
    qi = qi_tab[t]
    ki = ki_tab[t]

    @pl.when(ki == 0)
    def _():
        qs_ref[...] = (q_ref[...] * scale).astype(BF16)
        for m_ref, l_ref, acc_ref in ((m0_ref, l0_ref, acc0_ref), (m1_ref, l1_ref, acc1_ref)):
            m_ref[...] = jnp.full(m_ref.shape, NEG_INF, F32)
            l_ref[...] = jnp.zeros(l_ref.shape, F32)
            acc_ref[...] = jnp.zeros(acc_ref.shape, F32)

    def update(s, vt, m_ref, l_ref, acc_ref):
        m_prev = m_ref[...]
        m_new = jnp.maximum(m_prev, jnp.max(s, axis=0, keepdims=True))
        alpha = jnp.exp2(m_prev - m_new)
        p = jnp.exp2(s - m_new)
        l_ref[...] = alpha * l_ref[...] + jnp.sum(p, axis=0, keepdims=True)
        acc_ref[...] = alpha * acc_ref[...] + jnp.dot(vt, p.astype(BF16),
                                                      preferred_element_type=F32)
        m_ref[...] = m_new

    def step(diagonal):
        k = k_ref[...]
        lo_k = lax.broadcasted_iota(jnp.int32, (tk, LANES), 1) < HEAD_DIM
        k0 = jnp.where(lo_k, k, 0.0).astype(BF16)
        k1 = jnp.where(lo_k, 0.0, k).astype(BF16)
        s0 = lax.dot_general(k0, qs_ref[...], _NT, preferred_element_type=F32)
        s1 = lax.dot_general(k1, qs_ref[...], _NT, preferred_element_type=F32)
        if mode == "fox":
            s0 = s0 + cq_ref[0:1, :] - ck_ref[:, 0:1]
            s1 = s1 + cq_ref[1:2, :] - ck_ref[:, 1:2]
        if diagonal:
            key = lax.broadcasted_iota(jnp.int32, (tk, tq), 0)
            qry = lax.broadcasted_iota(jnp.int32, (tk, tq), 1)
            if mode == "fox":
                visible = key <= qry
            else:
                visible = (key // CHUNK) <= (qry // CHUNK)
            s0 = jnp.where(visible, s0, NEG_INF)
            s1 = jnp.where(visible, s1, NEG_INF)
        vt = v_ref[...].T.astype(BF16)
        update(s0, vt, m0_ref, l0_ref, acc0_ref)
        update(s1, vt, m1_ref, l1_ref, acc1_ref)

    @pl.when(ki < qi)
    def _():
        step(False)

    @pl.when(ki == qi)
    def _():
        step(True)
        o0 = acc0_ref[...] / l0_ref[...]
        o1 = acc1_ref[...] / l1_ref[...]
        if mode == "fox":
            first_head = lax.broadcasted_iota(jnp.int32, (LANES, tq), 0) < HEAD_DIM
            ot = jnp.where(first_head, o0, o1)
            o_ref[...] = ot.T * jax.nn.sigmoid(gate_ref[...])
        else:
            ot = o0 - _diff_lambda(lam_ref, lam_init) * o1
            y = ot * lax.rsqrt(jnp.mean(ot * ot, axis=0, keepdims=True) + EPS)
            o_ref[...] = (y.T * subg_ref[...]) * (1.0 - lam_init)


def prompt_attention(mode, q, k, vsrc, v_col0, batch, seq, tq, *, c_rows=None, c_cols=None,
                     gate_src=None, gate_col0=None, lam=None, sub_g=None, lam_init=0.0):
    W = q.shape[1]
    npair = W // LANES
    nq = seq // tq
    tri = [(a, b) for a in range(nq) for b in range(a + 1)]
    qi_tab = jnp.asarray([a for a, _ in tri], jnp.int32)
    ki_tab = jnp.asarray([b for _, b in tri], jnp.int32)
    qmap = lambda b, h, t, qt, kt: (b * nq + qt[t], h)
    kmap = lambda b, h, t, qt, kt: (b * nq + kt[t], h)
    in_specs = [pl.BlockSpec((tq, LANES), qmap),
                pl.BlockSpec((tq, LANES), kmap),
                pl.BlockSpec((tq, LANES), lambda b, h, t, qt, kt: (b * nq + kt[t], v_col0 + h))]
    args = [q, k, vsrc]
    if mode == "fox":
        in_specs += [pl.BlockSpec((None, None, 2, tq), lambda b, h, t, qt, kt: (b, h, 0, qt[t])),
                     pl.BlockSpec((None, None, tq, 2), lambda b, h, t, qt, kt: (b, h, kt[t], 0)),
                     pl.BlockSpec((tq, LANES),
                                  lambda b, h, t, qt, kt: (b * nq + qt[t], gate_col0 + h))]
        args += [c_rows, c_cols, gate_src]
    else:
        in_specs += [pl.BlockSpec(lam.shape, lambda b, h, t, qt, kt: (0, 0)),
                     pl.BlockSpec((1, LANES), lambda b, h, t, qt, kt: (0, 0))]
        args += [lam, sub_g]
    stat = pltpu.VMEM((1, tq), F32)
    acc = pltpu.VMEM((LANES, tq), F32)
    grid_spec = pltpu.PrefetchScalarGridSpec(
        num_scalar_prefetch=2,
        grid=(batch, npair, len(tri)),
        in_specs=in_specs,
        out_specs=pl.BlockSpec((tq, LANES), qmap),
        scratch_shapes=[pltpu.VMEM((tq, LANES), BF16), stat, stat, stat, stat, acc, acc],
    )
    return pl.pallas_call(
        functools.partial(_prompt_attn_kernel, mode=mode, lam_init=lam_init,
                          scale=LOG2E / math.sqrt(HEAD_DIM)),
        grid_spec=grid_spec,
        out_shape=jax.ShapeDtypeStruct((batch * seq, W), F32),
        compiler_params=_params(("arbitrary", "arbitrary", "arbitrary")),
        name=f"prompt_attention_{mode}",
    )(qi_tab, ki_tab, *args)


def _sample_attn_kernel(q_ref, kc_ref, vc_ref, kn_ref, vn_ref, *rest, mode, lam_init, scale,
                        past, v_time_minor):
    if mode == "fox":
        cq_ref, ckc_ref, ckn_ref, gate_ref, o_ref = rest[:5]
        scratch = rest[5:]
    else:
        lam_ref, subg_ref, o_ref = rest[:3]
        scratch = rest[3:]
    qbd_ref, m_ref, l_ref, acc_ref = scratch
    tnew, W = q_ref.shape
    nmap = W // HEAD_DIM
    R = nmap * tnew
    kstep = pl.program_id(1)
    nk = pl.num_programs(1) - 1

    @pl.when(kstep == 0)
    def _():
        q = q_ref[...] * scale
        qt = jnp.concatenate([q] * nmap, axis=0)
        rmap = lax.broadcasted_iota(jnp.int32, (R, W), 0) // tnew
        cmap = lax.broadcasted_iota(jnp.int32, (R, W), 1) // HEAD_DIM
        qbd_ref[...] = jnp.where(rmap == cmap, qt, 0.0).astype(BF16)
        m_ref[...] = jnp.full(m_ref.shape, NEG_INF, F32)
        l_ref[...] = jnp.zeros(l_ref.shape, F32)
        acc_ref[...] = jnp.zeros(acc_ref.shape, F32)

    def expand_rows(c):
        n = c.shape[1]
        return jnp.broadcast_to(c[:, None, :], (nmap, tnew, n)).reshape(R, n)

    def update(s, pv):
        m_prev = m_ref[...]
        m_new = jnp.maximum(m_prev, jnp.max(s, axis=-1, keepdims=True))
        alpha = jnp.exp(m_prev - m_new)
        p = jnp.exp(s - m_new)
        l_ref[...] = alpha * l_ref[...] + jnp.sum(p, axis=-1, keepdims=True)
        acc_ref[...] = alpha * acc_ref[...] + pv(p.astype(BF16))
        m_ref[...] = m_new

    @pl.when(kstep < nk)
    def _():
        s = jnp.dot(qbd_ref[...], kc_ref[...].astype(BF16), preferred_element_type=F32)
        if mode == "fox":
            s = s + cq_ref[...] - expand_rows(ckc_ref[...])
        if v_time_minor:
            v = vc_ref[...].astype(BF16)
            update(s, lambda p: lax.dot_general(p, v, _NT, preferred_element_type=F32))
        else:
            v = jnp.concatenate([vc_ref[:, h, :] for h in range(vc_ref.shape[1])],
                                axis=1).astype(BF16)
            update(s, lambda p: jnp.dot(p, v, preferred_element_type=F32))

    @pl.when(kstep == nk)
    def _():
        npad = LANES - tnew
        k = jnp.concatenate([kn_ref[...], jnp.zeros((npad, W), F32)], axis=0).astype(BF16)
        v = jnp.concatenate([vn_ref[...], jnp.zeros((npad, W), F32)], axis=0).astype(BF16)
        s = lax.dot_general(qbd_ref[...], k, _NT, preferred_element_type=F32)
        if mode == "fox":
            s = s + cq_ref[...] - expand_rows(ckn_ref[...])
        tpos = lax.broadcasted_iota(jnp.int32, (R, LANES), 0) % tnew
        spos = lax.broadcasted_iota(jnp.int32, (R, LANES), 1)
        if mode == "fox":
            visible = spos <= tpos
        else:
            visible = ((past + spos) // CHUNK) <= ((past + tpos) // CHUNK)
        visible = jnp.logical_and(visible, spos < tnew)
        update(jnp.where(visible, s, NEG_INF),
               lambda p: jnp.dot(p, v, preferred_element_type=F32))

        accn = acc_ref[...] / l_ref[...]
        if mode == "fox":
            lane_map = lax.broadcasted_iota(jnp.int32, (tnew, W), 1) // HEAD_DIM
            o = jnp.zeros((tnew, W), F32)
            for r in range(nmap):
                o = jnp.where(lane_map == r, accn[r * tnew:(r + 1) * tnew, :], o)
            o_ref[...] = o * jax.nn.sigmoid(gate_ref[...])
        else:
            lam = _diff_lambda(lam_ref, lam_init)
            for h in range(nmap // 2):
                sl = slice(h * LANES, (h + 1) * LANES)
                o = (accn[(2 * h) * tnew:(2 * h + 1) * tnew, sl]
                     - lam * accn[(2 * h + 1) * tnew:(2 * h + 2) * tnew, sl])
                y = o * lax.rsqrt(jnp.mean(o * o, axis=-1, keepdims=True) + EPS)
                o_ref[:, sl] = (y * subg_ref[...]) * (1.0 - lam_init)


def sample_attention(mode, q, k, vsrc, v_colblk, kcache_t, vcache, v_time_minor, layer_idx,
                     nstream, tnew, row0, tk, *, cq=None, ck_cache=None, ck_new=None,
                     gate_src=None, gate_colblk=None, lam=None, sub_g=None, lam_init=0.0):
    W = q.shape[1]
    past = kcache_t.shape[2]
    nk = past // tk
    nmap = W // HEAD_DIM
    R = nmap * tnew
    rb0 = row0 // tnew
    wblk = lambda colblk: pl.BlockSpec((tnew, W), lambda b, s, c=colblk: (rb0 + b, c))
    time_minor = pl.BlockSpec((None, W, tk),
                              lambda b, s: (layer_idx * nstream + b, 0, jnp.minimum(s, nk - 1)))
    time_major = pl.BlockSpec((None, tk) + vcache.shape[2:],
                              lambda b, s: (layer_idx * nstream + b, jnp.minimum(s, nk - 1), 0, 0))
    in_specs = [wblk(0), time_minor, time_minor if v_time_minor else time_major,
                wblk(0), wblk(v_colblk)]
    args = [q, kcache_t, vcache, k, vsrc]
    if mode == "fox":
        in_specs += [pl.BlockSpec((None, R, 1), lambda b, s: (b, 0, 0)),
                     pl.BlockSpec((None, nmap, tk), lambda b, s: (b, 0, jnp.minimum(s, nk - 1))),
                     pl.BlockSpec((None, nmap, LANES), lambda b, s: (b, 0, 0)),
                     wblk(gate_colblk)]
        args += [cq, ck_cache, ck_new, gate_src]
    else:
        in_specs += [pl.BlockSpec(lam.shape, lambda b, s: (0, 0)),
                     pl.BlockSpec((1, LANES), lambda b, s: (0, 0))]
        args += [lam, sub_g]
    return pl.pallas_call(
        functools.partial(_sample_attn_kernel, mode=mode, lam_init=lam_init,
                          scale=1.0 / math.sqrt(HEAD_DIM), past=past, v_time_minor=v_time_minor),
        grid=(nstream, nk + 1),
        in_specs=in_specs,
        out_specs=pl.BlockSpec((tnew, W), lambda b, s: (b, 0)),
        out_shape=jax.ShapeDtypeStruct((nstream * tnew, W), F32),
        scratch_shapes=[pltpu.VMEM((R, W), BF16), pltpu.VMEM((R, 1), F32),
                        pltpu.VMEM((R, 1), F32), pltpu.VMEM((R, W), F32)],
        compiler_params=_params(("arbitrary", "arbitrary")),
        name=f"sample_attention_{mode}",
    )(*args)


def _top16_desc(e):
    rows = []
    for _ in range(PEER_TOPK):
        m = jnp.max(e, axis=0, keepdims=True)
        rows.append(m)
        e = jnp.where(e == m, -1.0, e)
    return jnp.concatenate(rows, axis=0)


def _compare_exchange(v, i, l):
    v[i], v[l] = jnp.maximum(v[i], v[l]), jnp.minimum(v[i], v[l])


def _top16_ranks(e):
    n = e.shape[0] // SUBLANES
    assert n == PEER_TOPK
    v = [e[r * SUBLANES:(r + 1) * SUBLANES, :] for r in range(n)]
    k = 2
    while k <= n:
        j = k // 2
        while j >= 1:
            for i in range(n):
                l = i ^ j
                if l > i:
                    if i & k == 0:
                        _compare_exchange(v, i, l)
                    else:
                        _compare_exchange(v, l, i)
            j //= 2
        k *= 2
    shift = SUBLANES // 2
    while shift >= 1:
        w = [pltpu.roll(x, shift, axis=0) for x in v]
        v = [jnp.maximum(v[i], w[n - 1 - i]) for i in range(n)]
        j = n // 2
        while j >= 1:
            for i in range(n):
                l = i ^ j
                if l > i:
                    _compare_exchange(v, i, l)
            j //= 2
        shift //= 2
    return v


def _ranks_on_sublanes(v, first):
    row = lax.broadcasted_iota(jnp.int32, v[0].shape, 0)
    out = v[first + SUBLANES - 1]
    for r in range(SUBLANES - 2, -1, -1):
        out = jnp.where(row == r, v[first + r], out)
    return out


def _pair_candidates(a, b):
    row = lax.broadcasted_iota(jnp.int32, a[0].shape, 0)
    b_lo = _ranks_on_sublanes(b, 0)
    b_hi = _ranks_on_sublanes(b, SUBLANES)
    parts = [b_lo * a[0], b_hi * a[0], b_lo * a[1]]
    for r1 in range(2, SUBLANES):
        parts.append(jnp.where(row < PEER_TOPK // (r1 + 1), b_lo * a[r1], -1.0))
    parts.append(b[0] * _ranks_on_sublanes(a, SUBLANES))
    return jnp.concatenate(parts, axis=0)


def _peer_kernel(x_ref, g_ref, wq_ref, sk_ref, u0_ref, un_ref, vp_ref, vl_ref, o_ref,
                 hn_ref, e1_ref, e2_ref, thr_ref, at0_ref, at1_ref, wt0_ref, wt1_ref, yt_ref,
                 *, nheads):
    tb, D = x_ref.shape
    te = un_ref.shape[0]
    nkeys = sk_ref.shape[1]
    half = sk_ref.shape[2]
    e_step = pl.program_id(1)
    ne = pl.num_programs(1)
    assert te == SUBLANES * nkeys
    chunk_w = min(MXU_WIDTH, tb)
    nchunk = tb // chunk_w

    @pl.when(e_step == 0)
    def _():
        x = x_ref[...]
        y = x * lax.rsqrt(jnp.mean(x * x, axis=-1, keepdims=True) + EPS)
        hn_ref[...] = (y * g_ref[...]).astype(BF16)

        def head_body(h, carry):
            es = []
            for c in range(2):
                r0 = pl.multiple_of((h * 2 + c) * half, half)
                qs = lax.dot_general(wq_ref[pl.ds(r0, half), :], hn_ref[...], _NT,
                                     preferred_element_type=F32).astype(BF16)
                s = jnp.dot(sk_ref[h * 2 + c], qs, preferred_element_type=F32)
                es.append(jnp.exp(s - jnp.max(s, axis=0, keepdims=True)))
            e1, e2 = es
            a = _top16_ranks(e1)
            b = _top16_ranks(e2)
            cand = _pair_candidates(a, b)
            top = jnp.maximum(_top16_desc(cand), 0.0)
            rz = 1.0 / jnp.sum(top, axis=0, keepdims=True)
            e1n = e1 * rz
            tiny = jnp.finfo(F32).tiny
            winner = cand >= jnp.maximum(top[PEER_TOPK - 1:PEER_TOPK, :], tiny)
            cand_n = _pair_candidates([ar * rz for ar in a], b)
            thr = jnp.min(jnp.where(winner, cand_n, 2.0), axis=0, keepdims=True)
            thr = jnp.maximum(thr, tiny)
            e1_ref[h] = e1n
            e2_ref[h] = e2
            thr_ref[h] = jnp.broadcast_to(thr, (SUBLANES, tb))
            return carry

        lax.fori_loop(0, nheads, head_body, 0)
        at0_ref[...] = lax.dot_general(u0_ref[...], hn_ref[...], _NT, preferred_element_type=F32)
        wt1_ref[...] = jnp.zeros(wt1_ref.shape, BF16)
        yt_ref[...] = jnp.zeros(yt_ref.shape, F32)

    def mxu_chunk(c, at_nxt, wt_prv):
        cols = slice((c % nchunk) * chunk_w, (c % nchunk + 1) * chunk_w)
        if c < nchunk:
            at_nxt[:, cols] = lax.dot_general(un_ref[...], hn_ref[cols, :], _NT,
                                              preferred_element_type=F32)
        else:
            yt_ref[:, cols] += jnp.dot(vp_ref[...], wt_prv[:, cols],
                                       preferred_element_type=F32)

    def body(at_cur, at_nxt, wt_cur, wt_prv):
        i0 = pl.multiple_of(e_step * SUBLANES, SUBLANES)
        n_gate = tb // LANES
        n_mxu = 2 * nchunk
        for tc in range(n_gate):
            for c in range(tc * n_mxu // n_gate, (tc + 1) * n_mxu // n_gate):
                mxu_chunk(c, at_nxt, wt_prv)
            sl = slice(tc * LANES, (tc + 1) * LANES)
            e1_tiles = [e1_ref[h, pl.ds(i0, SUBLANES), sl] for h in range(nheads)]
            for ii in range(SUBLANES):
                for jh in range(nkeys // GATE_ROWS):
                    js = slice(jh * GATE_ROWS, (jh + 1) * GATE_ROWS)
                    gate = jnp.zeros((GATE_ROWS, LANES), F32)
                    for h in range(nheads):
                        t = e2_ref[h, js, sl] * e1_tiles[h][ii:ii + 1, :]
                        gate = gate + jnp.where(t >= thr_ref[h, 0:1, sl], t, 0.0)
                    rows = slice(ii * nkeys + jh * GATE_ROWS, ii * nkeys + (jh + 1) * GATE_ROWS)
                    a = at_cur[rows, sl]
                    act = 0.5 * a * (1.0 + lax.erf(a * (1.0 / math.sqrt(2.0))))
                    wt_cur[rows, sl] = (gate * act).astype(BF16)

    @pl.when(e_step % 2 == 0)
    def _():
        body(at0_ref, at1_ref, wt0_ref, wt1_ref)

    @pl.when(e_step % 2 == 1)
    def _():
        body(at1_ref, at0_ref, wt1_ref, wt0_ref)

    @pl.when(e_step == ne - 1)
    def _():
        yt = yt_ref[...] + jnp.dot(vl_ref[...], wt1_ref[...], preferred_element_type=F32)
        o_ref[...] = x_ref[...] + yt.T


def peer_layer(x, g, wq_t, sub_keys, u, v_t, tb):
    M, D = x.shape
    E = u.shape[0]
    nheads = sub_keys.shape[0] // 2
    nkeys = sub_keys.shape[1]
    te = SUBLANES * nkeys
    ne = E // te
    assert ne % 2 == 0
    return pl.pallas_call(
        functools.partial(_peer_kernel, nheads=nheads),
        grid=(M // tb, ne),
        in_specs=[pl.BlockSpec((tb, D), lambda i, e: (i, 0)),
                  pl.BlockSpec((1, D), lambda i, e: (0, 0)),
                  pl.BlockSpec(wq_t.shape, lambda i, e: (0, 0)),
                  pl.BlockSpec(sub_keys.shape, lambda i, e: (0, 0, 0)),
                  pl.BlockSpec((te, D), lambda i, e: (0, 0)),
                  pl.BlockSpec((te, D), lambda i, e: (jnp.minimum(e + 1, ne - 1), 0)),
                  pl.BlockSpec((D, te), lambda i, e: (0, jnp.maximum(e - 1, 0))),
                  pl.BlockSpec((D, te), lambda i, e: (0, ne - 1))],
        out_specs=pl.BlockSpec((tb, D), lambda i, e: (i, 0)),
        out_shape=jax.ShapeDtypeStruct((M, D), F32),
        scratch_shapes=[pltpu.VMEM((tb, D), BF16),
                        pltpu.VMEM((nheads, nkeys, tb), F32),
                        pltpu.VMEM((nheads, nkeys, tb), F32),
                        pltpu.VMEM((nheads, SUBLANES, tb), F32),
                        pltpu.VMEM((te, tb), F32),
                        pltpu.VMEM((te, tb), F32),
                        pltpu.VMEM((te, tb), BF16),
                        pltpu.VMEM((te, tb), BF16),
                        pltpu.VMEM((D, tb), F32)],
        compiler_params=_params(("arbitrary", "arbitrary")),
        name="peer",
    )(x, g, wq_t, sub_keys, u, u, v_t, v_t)


def _rope_tables(pos):
    inv = ROPE_THETA ** (-jnp.arange(0, HEAD_DIM, 2, dtype=F32) / HEAD_DIM)
    ang = pos.astype(F32)[:, None] * inv[None, :]
    ang = jnp.concatenate([ang, ang, ang, ang], axis=-1)
    return jnp.cos(ang), jnp.sin(ang)


def _tile_heads(g, width):
    return jnp.tile(g, width // g.shape[0]).reshape(1, width)


def _time_minor(cache):
    nd = cache.ndim
    c = jnp.transpose(cache, (0, 1) + tuple(range(3, nd)) + (2,))
    return c.reshape(c.shape[0] * c.shape[1], -1, c.shape[-1])


def _forward(x_prompt, x_sample, cache_fox_k, cache_fox_v, cache_fox_logf, cache_diff_k,
             cache_diff_v, norm_mix_g, norm_ffn_g, fox_w_in, fox_b_f, fox_q_g, fox_k_g, fox_w_o,
             diff_w_in, diff_q_g, diff_k_g, diff_lambda, diff_sub_g, diff_w_o,
             peer_w_q, peer_sub_keys, peer_u, peer_v, *, tm, tq, tk_cache, tb):
    B, S, D = x_prompt.shape
    NS, T, _ = x_sample.shape
    depth = norm_mix_g.shape[0]
    past = cache_fox_k.shape[2]
    MP = B * S
    M = MP + NS * T
    fox_heads = cache_fox_k.shape[3]
    diff_heads = cache_diff_k.shape[3]
    W = fox_heads * HEAD_DIM
    npair = W // LANES

    x = jnp.concatenate([x_prompt.reshape(MP, D), x_sample.reshape(NS * T, D)], axis=0)

    pos = jnp.concatenate([jnp.tile(jnp.arange(S, dtype=jnp.int32), B),
                           jnp.tile(past + jnp.arange(T, dtype=jnp.int32), NS)])
    cos, sin = _rope_tables(pos)

    kc_fox = _time_minor(cache_fox_k)
    vc_fox = _time_minor(cache_fox_v)
    kc_diff = _time_minor(cache_diff_k)
    vc_diff = cache_diff_v.reshape((-1, past) + cache_diff_v.shape[3:])
    lf_cache = jnp.swapaxes(cache_fox_logf, 2, 3)

    outs = {k: [] for k in ("fk_p", "fv_p", "fl_p", "dk_p", "dv_p",
                            "fk_s", "fv_s", "fl_s", "dk_s", "dv_s")}

    for layer in range(depth):
        j = layer // 2
        g_mix = norm_mix_g[layer].reshape(1, D)
        if layer % 2 == 0:
            w_in = fox_w_in[j]
            w_main = w_in[:, :4 * W].astype(BF16)
            w_f = jnp.pad(w_in[:, 4 * W:], ((0, 0), (0, LANES - fox_heads))).astype(BF16)
            p = norm_matmul(x, g_mix, w_main, tm, 1024)
            pf = norm_matmul(x, g_mix, w_f, tm, LANES)
            b_f = jnp.pad(fox_b_f[j], (0, LANES - fox_heads)).reshape(1, LANES)
            qn, kn, lf, lf_t = fox_post(p, pf, b_f, _tile_heads(fox_q_g[j], W),
                                        _tile_heads(fox_k_g[j], W), tm)
            lf = lf[:, :fox_heads]
            lf_p = lf[:MP].reshape(B, S, fox_heads)
            lf_s = lf[MP:].reshape(NS, T, fox_heads)
            c_t = cumsum_lanes(lf_t[:, :MP], LANES, S)[:fox_heads]
            c_pt = jnp.swapaxes(c_t.reshape(fox_heads, B, S), 0, 1)
            c_rows = (c_pt * LOG2E).reshape(B, npair, 2, S)
            c_cols = jnp.swapaxes(c_rows, 2, 3)
            o_p = prompt_attention("fox", qn, kn, p, 2 * npair, B, S, tq,
                                   c_rows=c_rows, c_cols=c_cols, gate_src=p,
                                   gate_col0=3 * npair)
            c_cache = cumsum_lanes(lf_cache[j].reshape(NS * fox_heads, past), LANES, past)
            lf_new = jnp.pad(jnp.swapaxes(lf_s, 1, 2), ((0, 0), (0, 0), (0, LANES - T)))
            c_new = cumsum_lanes(lf_new.reshape(NS * fox_heads, LANES), LANES, LANES)
            ck_new = (c_cache[:, past - 1:] + c_new).reshape(NS, fox_heads, LANES)
            cq_s = ck_new[:, :, :T].reshape(NS, fox_heads * T, 1)
            o_s = sample_attention("fox", qn, kn, p, 2, kc_fox, vc_fox, True, j, NS, T, MP,
                                   tk_cache, cq=cq_s,
                                   ck_cache=c_cache.reshape(NS, fox_heads, past), ck_new=ck_new,
                                   gate_src=p, gate_colblk=3)
            o = jnp.concatenate([o_p, o_s], axis=0)
            x = matmul_residual(o, fox_w_o[j].astype(BF16), x, tm)
            vv = p[:, 2 * W:3 * W]
            outs["fk_p"].append(kn[:MP].reshape(B, S, fox_heads, HEAD_DIM))
            outs["fv_p"].append(vv[:MP].reshape(B, S, fox_heads, HEAD_DIM))
            outs["fl_p"].append(lf_p)
            outs["fk_s"].append(kn[MP:].reshape(NS, T, fox_heads, HEAD_DIM))
            outs["fv_s"].append(vv[MP:].reshape(NS, T, fox_heads, HEAD_DIM))
            outs["fl_s"].append(lf_s)
        else:
            lam_init = 0.8 - 0.6 * math.exp(-0.3 * layer)
            p = norm_matmul(x, g_mix, diff_w_in[j].astype(BF16), tm, 1024)
            qn, kn = diff_post(p, cos, sin, _tile_heads(diff_q_g[j], W),
                               _tile_heads(diff_k_g[j], W), tm)
            sub_g = diff_sub_g[j].reshape(1, LANES)
            o_p = prompt_attention("diff", qn, kn, p, 2 * npair, B, S, tq,
                                   lam=diff_lambda[j], sub_g=sub_g, lam_init=lam_init)
            o_s = sample_attention("diff", qn, kn, p, 2, kc_diff, vc_diff, False, j, NS, T, MP,
                                   tk_cache, lam=diff_lambda[j], sub_g=sub_g, lam_init=lam_init)
            o = jnp.concatenate([o_p, o_s], axis=0)
            x = matmul_residual(o, diff_w_o[j].astype(BF16), x, tm)
            vv = p[:, 2 * W:3 * W]
            outs["dk_p"].append(kn[:MP].reshape(B, S, diff_heads, 2, HEAD_DIM))
            outs["dv_p"].append(vv[:MP].reshape(B, S, diff_heads, 2 * HEAD_DIM))
            outs["dk_s"].append(kn[MP:].reshape(NS, T, diff_heads, 2, HEAD_DIM))
            outs["dv_s"].append(vv[MP:].reshape(NS, T, diff_heads, 2 * HEAD_DIM))

        sk = peer_sub_keys[layer]
        x = peer_layer(x, norm_ffn_g[layer].reshape(1, D),
                       peer_w_q[layer].T.astype(BF16),
                       sk.reshape(sk.shape[0] * 2, sk.shape[2], sk.shape[3]).astype(BF16),
                       peer_u[layer].astype(BF16), peer_v[layer].T.astype(BF16), tb)

    return (x[:MP].reshape(B, S, D), x[MP:].reshape(NS, T, D),
            jnp.stack(outs["fk_p"]), jnp.stack(outs["fv_p"]), jnp.stack(outs["fl_p"]),
            jnp.stack(outs["dk_p"]), jnp.stack(outs["dv_p"]),
            jnp.stack(outs["fk_s"]), jnp.stack(outs["fv_s"]), jnp.stack(outs["fl_s"]),
            jnp.stack(outs["dk_s"]), jnp.stack(outs["dv_s"]))


def kernel(x_prompt, x_sample, cache_fox_k, cache_fox_v, cache_fox_logf, cache_diff_k, cache_diff_v, norm_mix_g, norm_ffn_g, fox_w_in, fox_b_f, fox_q_g, fox_k_g, fox_w_o, diff_w_in, diff_q_g, diff_k_g, diff_lambda, diff_sub_g, diff_w_o, peer_w_q, peer_sub_keys, peer_u, peer_v):
    return _forward(x_prompt, x_sample, cache_fox_k, cache_fox_v, cache_fox_logf, cache_diff_k,
                    cache_diff_v, norm_mix_g, norm_ffn_g, fox_w_in, fox_b_f, fox_q_g, fox_k_g,
                    fox_w_o, diff_w_in, diff_q_g, diff_k_g, diff_lambda, diff_sub_g, diff_w_o,
                    peer_w_q, peer_sub_keys, peer_u, peer_v,
                    tm=512, tq=512, tk_cache=512, tb=512)
```

```python
import functools
import math

import jax
import jax.numpy as jnp
from jax import lax
from jax.experimental import pallas as pl
from jax.experimental.pallas import tpu as pltpu

F32 = jnp.float32
BF16 = jnp.bfloat16
EPS = 1e-6
NEG_INF = -1e30
CHUNK = 64
ROPE_THETA = 10000.0
LOG2E = math.log2(math.e)
PEER_TOPK = 16
LANES = 128
SUBLANES = 8
MXU_WIDTH = 256
GATE_ROWS = 32
PIECE_ROWS = 256
HEAD_DIM = 64
VMEM_LIMIT = 56 * 1024 * 1024

_NT = (((1,), (1,)), ((), ()))


def _params(sem, vmem=VMEM_LIMIT):
    return pltpu.CompilerParams(dimension_semantics=sem, vmem_limit_bytes=vmem)


def _norm_matmul_kernel(x_ref, g_ref, w_ref, o_ref, hn_ref):
    @pl.when(pl.program_id(1) == 0)
    def _():
        x = x_ref[...]
        y = x * lax.rsqrt(jnp.mean(x * x, axis=-1, keepdims=True) + EPS)
        hn_ref[...] = (y * g_ref[...]).astype(BF16)

    o_ref[...] = jnp.dot(hn_ref[...], w_ref[...], preferred_element_type=F32)


def norm_matmul(x, g, w, tm, tn):
    M, D = x.shape
    N = w.shape[1]
    return pl.pallas_call(
        _norm_matmul_kernel,
        grid=(M // tm, N // tn),
        in_specs=[pl.BlockSpec((tm, D), lambda i, j: (i, 0)),
                  pl.BlockSpec((1, D), lambda i, j: (0, 0)),
                  pl.BlockSpec((D, tn), lambda i, j: (0, j))],
        out_specs=pl.BlockSpec((tm, tn), lambda i, j: (i, j)),
        out_shape=jax.ShapeDtypeStruct((M, N), F32),
        scratch_shapes=[pltpu.VMEM((tm, D), BF16)],
        compiler_params=_params(("arbitrary", "arbitrary")),
        name="norm_matmul",
    )(x, g, w)


def _matmul_res_kernel(a_ref, w_ref, r_ref, o_ref):
    o_ref[...] = r_ref[...] + jnp.dot(a_ref[...].astype(BF16), w_ref[...],
                                      preferred_element_type=F32)


def matmul_residual(a, w, res, tm):
    M, K = a.shape
    N = w.shape[1]
    return pl.pallas_call(
        _matmul_res_kernel,
        grid=(M // tm,),
        in_specs=[pl.BlockSpec((tm, K), lambda i: (i, 0)),
                  pl.BlockSpec((K, N), lambda i: (0, 0)),
                  pl.BlockSpec((tm, N), lambda i: (i, 0))],
        out_specs=pl.BlockSpec((tm, N), lambda i: (i, 0)),
        out_shape=jax.ShapeDtypeStruct((M, N), F32),
        compiler_params=_params(("arbitrary",)),
        name="matmul_residual",
    )(a, w, res)


def _emit_kernel(src_ref, *rest, transpose):
    o_ref = rest[-1]
    o_ref[...] = src_ref[...].T if transpose else src_ref[...]


def emit_prompt_rows(src, colblk, width, layer, nlayers, batch, seq, tm, transpose, buf=None):
    nblk = seq // tm
    in_specs = [pl.BlockSpec((tm, width), lambda b, s: (b * nblk + s, colblk))]
    args = [src]
    if buf is not None:
        in_specs.append(pl.BlockSpec(memory_space=pl.ANY))
        args.append(buf)
    if transpose:
        out_shape = (nlayers, batch, width, seq)
        out_spec = pl.BlockSpec((None, None, width, tm), lambda b, s: (layer, b, 0, s))
    else:
        out_shape = (nlayers, batch, seq, width)
        out_spec = pl.BlockSpec((None, None, tm, width), lambda b, s: (layer, b, s, 0))
    return pl.pallas_call(
        functools.partial(_emit_kernel, transpose=transpose),
        grid=(batch, nblk),
        in_specs=in_specs,
        out_specs=out_spec,
        out_shape=jax.ShapeDtypeStruct(out_shape, F32),
        input_output_aliases={} if buf is None else {1: 0},
        compiler_params=_params(("arbitrary", "arbitrary")),
        name="emit_prompt_rows",
    )(*args)


def _head_rms64(xc, lo):
    sq = xc * xc
    s_lo = jnp.sum(jnp.where(lo, sq, 0.0), axis=-1, keepdims=True)
    s_hi = jnp.sum(jnp.where(lo, 0.0, sq), axis=-1, keepdims=True)
    ms = jnp.where(lo, s_lo, s_hi) * (1.0 / HEAD_DIM)
    return lax.rsqrt(ms + EPS)


def _fox_post_kernel(q_ref, k_ref, f_ref, bf_ref, qg_ref, kg_ref, qo_ref, ko_ref, lf_ref,
                     lft_ref):
    tm, width = q_ref.shape
    lo = lax.broadcasted_iota(jnp.int32, (tm, LANES), 1) < HEAD_DIM
    for src, g_ref, dst in ((q_ref, qg_ref, qo_ref), (k_ref, kg_ref, ko_ref)):
        for c in range(width // LANES):
            sl = slice(c * LANES, (c + 1) * LANES)
            xc = src[:, sl]
            dst[:, sl] = xc * _head_rms64(xc, lo) * g_ref[:, sl]
    z = f_ref[...] + bf_ref[...]
    lf = jnp.minimum(z, 0.0) - jnp.log1p(jnp.exp(-jnp.abs(z)))
    lf_ref[...] = lf
    lft_ref[...] = lf.T


def fox_post(p, pf, b_f, q_g, k_g, tm):
    M = p.shape[0]
    W = q_g.shape[1]
    row = lambda c: pl.BlockSpec((tm, W), lambda i, c=c: (i, c))
    vec = lambda n: pl.BlockSpec((1, n), lambda i: (0, 0))
    return pl.pallas_call(
        _fox_post_kernel,
        grid=(M // tm,),
        in_specs=[row(0), row(1), pl.BlockSpec((tm, LANES), lambda i: (i, 0)),
                  vec(LANES), vec(W), vec(W)],
        out_specs=[row(0), row(0), pl.BlockSpec((tm, LANES), lambda i: (i, 0)),
                   pl.BlockSpec((LANES, tm), lambda i: (0, i))],
        out_shape=[jax.ShapeDtypeStruct((M, W), F32), jax.ShapeDtypeStruct((M, W), F32),
                   jax.ShapeDtypeStruct((M, LANES), F32), jax.ShapeDtypeStruct((LANES, M), F32)],
        compiler_params=_params(("arbitrary",)),
        name="fox_post",
    )(p, p, pf, b_f, q_g, k_g)


def _diff_post_kernel(q_ref, k_ref, cos_ref, sin_ref, qg_ref, kg_ref, qo_ref, ko_ref):
    tm, width = q_ref.shape
    lane = lax.broadcasted_iota(jnp.int32, (tm, LANES), 1)
    lo = lane < HEAD_DIM
    first_half = (lane % HEAD_DIM) < (HEAD_DIM // 2)
    cos = cos_ref[...]
    sin = sin_ref[...]
    for src, g_ref, dst in ((q_ref, qg_ref, qo_ref), (k_ref, kg_ref, ko_ref)):
        for c in range(width // LANES):
            sl = slice(c * LANES, (c + 1) * LANES)
            xc = src[:, sl]
            xn = xc * _head_rms64(xc, lo) * g_ref[:, sl]
            rot = jnp.where(first_half,
                            -pltpu.roll(xn, LANES - HEAD_DIM // 2, axis=1),
                            pltpu.roll(xn, HEAD_DIM // 2, axis=1))
            dst[:, sl] = xn * cos + rot * sin


def diff_post(p, cos, sin, q_g, k_g, tm):
    M = p.shape[0]
    W = q_g.shape[1]
    row = lambda c: pl.BlockSpec((tm, W), lambda i, c=c: (i, c))
    tab = pl.BlockSpec((tm, LANES), lambda i: (i, 0))
    vec = pl.BlockSpec((1, W), lambda i: (0, 0))
    return pl.pallas_call(
        _diff_post_kernel,
        grid=(M // tm,),
        in_specs=[row(0), row(1), tab, tab, vec, vec],
        out_specs=[row(0), row(0)],
        out_shape=[jax.ShapeDtypeStruct((M, W), F32), jax.ShapeDtypeStruct((M, W), F32)],
        compiler_params=_params(("arbitrary",)),
        name="diff_post",
    )(p, p, cos, sin, q_g, k_g)


def _cumsum_kernel(x_ref, o_ref):
    rows, length = x_ref.shape
    s_idx = lax.broadcasted_iota(jnp.int32, (LANES, LANES), 0)
    t_idx = lax.broadcasted_iota(jnp.int32, (LANES, LANES), 1)
    tri = (s_idx <= t_idx).astype(F32)
    carry = jnp.zeros((rows, 1), F32)
    for blk in range(length // LANES):
        sl = slice(blk * LANES, (blk + 1) * LANES)
        c = jnp.dot(x_ref[:, sl], tri, preferred_element_type=F32,
                    precision=lax.Precision.HIGHEST) + carry
        o_ref[:, sl] = c
        carry = c[:, LANES - 1:LANES]


def cumsum_lanes(x, rows, length):
    R, L = x.shape
    return pl.pallas_call(
        _cumsum_kernel,
        grid=(R // rows, L // length),
        in_specs=[pl.BlockSpec((rows, length), lambda i, j: (i, j))],
        out_specs=pl.BlockSpec((rows, length), lambda i, j: (i, j)),
        out_shape=jax.ShapeDtypeStruct((R, L), F32),
        compiler_params=_params(("arbitrary", "arbitrary")),
        name="cumsum_lanes",
    )(x)


def _diff_lambda(lam_ref, lam_init):
    lf = lam_ref[...]
    a = jnp.sum(lf[0:1, :] * lf[1:2, :], axis=-1, keepdims=True)
    b = jnp.sum(lf[2:3, :] * lf[3:4, :], axis=-1, keepdims=True)
    return jnp.exp(a) - jnp.exp(b) + lam_init


def _prompt_attn_kernel(qi_tab, ki_tab, q_ref, k_ref, v_ref, *rest, mode, lam_init, scale):
    if mode == "fox":
        cq_ref, ck_ref, gate_ref, o_ref = rest[:4]
        scratch = rest[4:]
    else:
        lam_ref, subg_ref, o_ref = rest[:3]
        scratch = rest[3:]
    qs_ref, m0_ref, m1_ref, l0_ref, l1_ref, acc0_ref, acc1_ref = scratch
    tq = q_ref.shape[0]
    tk = k_ref.shape[0]
    t = pl.program_id(2)
    qi = qi_tab[t]
    ki = ki_tab[t]

    @pl.when(ki == 0)
    def _():
        qs_ref[...] = (q_ref[...] * scale).astype(BF16)
        for m_ref, l_ref, acc_ref in ((m0_ref, l0_ref, acc0_ref), (m1_ref, l1_ref, acc1_ref)):
            m_ref[...] = jnp.full(m_ref.shape, NEG_INF, F32)
            l_ref[...] = jnp.zeros(l_ref.shape, F32)
            acc_ref[...] = jnp.zeros(acc_ref.shape, F32)

    def update(s, vt, m_ref, l_ref, acc_ref):
        m_prev = m_ref[...]
        m_new = jnp.maximum(m_prev, jnp.max(s, axis=0, keepdims=True))
        alpha = jnp.exp2(m_prev - m_new)
        p = jnp.exp2(s - m_new)
        l_ref[...] = alpha * l_ref[...] + jnp.sum(p, axis=0, keepdims=True)
        acc_ref[...] = alpha * acc_ref[...] + jnp.dot(vt, p.astype(BF16),
                                                      preferred_element_type=F32)
        m_ref[...] = m_new

    def step(diagonal):
        k = k_ref[...]
        lo_k = lax.broadcasted_iota(jnp.int32, (tk, LANES), 1) < HEAD_DIM
        k0 = jnp.where(lo_k, k, 0.0).astype(BF16)
        k1 = jnp.where(lo_k, 0.0, k).astype(BF16)
        s0 = lax.dot_general(k0, qs_ref[...], _NT, preferred_element_type=F32)
        s1 = lax.dot_general(k1, qs_ref[...], _NT, preferred_element_type=F32)
        if mode == "fox":
            s0 = s0 + cq_ref[0:1, :] - ck_ref[:, 0:1]
            s1 = s1 + cq_ref[1:2, :] - ck_ref[:, 1:2]
        if diagonal:
            key = lax.broadcasted_iota(jnp.int32, (tk, tq), 0)
            qry = lax.broadcasted_iota(jnp.int32, (tk, tq), 1)
            if mode == "fox":
                visible = key <= qry
            else:
                visible = (key // CHUNK) <= (qry // CHUNK)
            s0 = jnp.where(visible, s0, NEG_INF)
            s1 = jnp.where(visible, s1, NEG_INF)
        vt = v_ref[...].T.astype(BF16)
        update(s0, vt, m0_ref, l0_ref, acc0_ref)
        update(s1, vt, m1_ref, l1_ref, acc1_ref)

    @pl.when(ki < qi)
    def _():
        step(False)

    @pl.when(ki == qi)
    def _():
        step(True)
        o0 = acc0_ref[...] / l0_ref[...]
        o1 = acc1_ref[...] / l1_ref[...]
        if mode == "fox":
            first_head = lax.broadcasted_iota(jnp.int32, (LANES, tq), 0) < HEAD_DIM
            ot = jnp.where(first_head, o0, o1)
            o_ref[...] = ot.T * jax.nn.sigmoid(gate_ref[...])
        else:
            ot = o0 - _diff_lambda(lam_ref, lam_init) * o1
            y = ot * lax.rsqrt(jnp.mean(ot * ot, axis=0, keepdims=True) + EPS)
            o_ref[...] = (y.T * subg_ref[...]) * (1.0 - lam_init)


def prompt_attention(mode, q, k, vsrc, v_col0, batch, seq, tq, *, c_rows=None, c_cols=None,
                     gate_src=None, gate_col0=None, lam=None, sub_g=None, lam_init=0.0):
    W = q.shape[1]
    npair = W // LANES
    nq = seq // tq
    tri = [(a, b) for a in range(nq) for b in range(a + 1)]
    qi_tab = jnp.asarray([a for a, _ in tri], jnp.int32)
    ki_tab = jnp.asarray([b for _, b in tri], jnp.int32)
    qmap = lambda b, h, t, qt, kt: (b * nq + qt[t], h)
    kmap = lambda b, h, t, qt, kt: (b * nq + kt[t], h)
    in_specs = [pl.BlockSpec((tq, LANES), qmap),
                pl.BlockSpec((tq, LANES), kmap),
                pl.BlockSpec((tq, LANES), lambda b, h, t, qt, kt: (b * nq + kt[t], v_col0 + h))]
    args = [q, k, vsrc]
    if mode == "fox":
        in_specs += [pl.BlockSpec((None, None, 2, tq), lambda b, h, t, qt, kt: (b, h, 0, qt[t])),
                     pl.BlockSpec((None, None, tq, 2), lambda b, h, t, qt, kt: (b, h, kt[t], 0)),
                     pl.BlockSpec((tq, LANES),
                                  lambda b, h, t, qt, kt: (b * nq + qt[t], gate_col0 + h))]
        args += [c_rows, c_cols, gate_src]
    else:
        in_specs += [pl.BlockSpec(lam.shape, lambda b, h, t, qt, kt: (0, 0)),
                     pl.BlockSpec((1, LANES), lambda b, h, t, qt, kt: (0, 0))]
        args += [lam, sub_g]
    stat = pltpu.VMEM((1, tq), F32)
    acc = pltpu.VMEM((LANES, tq), F32)
    grid_spec = pltpu.PrefetchScalarGridSpec(
        num_scalar_prefetch=2,
        grid=(batch, npair, len(tri)),
        in_specs=in_specs,
        out_specs=pl.BlockSpec((tq, LANES), qmap),
        scratch_shapes=[pltpu.VMEM((tq, LANES), BF16), stat, stat, stat, stat, acc, acc],
    )
    return pl.pallas_call(
        functools.partial(_prompt_attn_kernel, mode=mode, lam_init=lam_init,
                          scale=LOG2E / math.sqrt(HEAD_DIM)),
        grid_spec=grid_spec,
        out_shape=jax.ShapeDtypeStruct(q.shape, F32),
        compiler_params=_params(("arbitrary", "arbitrary", "arbitrary")),
        name=f"prompt_attention_{mode}",
    )(qi_tab, ki_tab, *args)


def _sample_attn_kernel(buf_ref, q_ref, kc_ref, vc_ref, kn_ref, vn_ref, *rest, mode, lam_init,
                        scale, past, v_time_minor):
    del buf_ref
    if mode == "fox":
        cq_ref, ckc_ref, ckn_ref, gate_ref, o_ref = rest[:5]
        scratch = rest[5:]
    else:
        lam_ref, subg_ref, o_ref = rest[:3]
        scratch = rest[3:]
    qbd_ref, m_ref, l_ref, acc_ref = scratch
    tnew, W = q_ref.shape
    nmap = W // HEAD_DIM
    R = nmap * tnew
    kstep = pl.program_id(1)
    nk = pl.num_programs(1) - 1

    @pl.when(kstep == 0)
    def _():
        q = q_ref[...] * scale
        qt = jnp.concatenate([q] * nmap, axis=0)
        rmap = lax.broadcasted_iota(jnp.int32, (R, W), 0) // tnew
        cmap = lax.broadcasted_iota(jnp.int32, (R, W), 1) // HEAD_DIM
        qbd_ref[...] = jnp.where(rmap == cmap, qt, 0.0).astype(BF16)
        m_ref[...] = jnp.full(m_ref.shape, NEG_INF, F32)
        l_ref[...] = jnp.zeros(l_ref.shape, F32)
        acc_ref[...] = jnp.zeros(acc_ref.shape, F32)

    def expand_rows(c):
        n = c.shape[1]
        return jnp.broadcast_to(c[:, None, :], (nmap, tnew, n)).reshape(R, n)

    def update(s, pv):
        m_prev = m_ref[...]
        m_new = jnp.maximum(m_prev, jnp.max(s, axis=-1, keepdims=True))
        alpha = jnp.exp(m_prev - m_new)
        p = jnp.exp(s - m_new)
        l_ref[...] = alpha * l_ref[...] + jnp.sum(p, axis=-1, keepdims=True)
        acc_ref[...] = alpha * acc_ref[...] + pv(p.astype(BF16))
        m_ref[...] = m_new

    @pl.when(kstep < nk)
    def _():
        s = jnp.dot(qbd_ref[...], kc_ref[...].astype(BF16), preferred_element_type=F32)
        if mode == "fox":
            s = s + cq_ref[...] - expand_rows(ckc_ref[...])
        if v_time_minor:
            v = vc_ref[...].astype(BF16)
            update(s, lambda p: lax.dot_general(p, v, _NT, preferred_element_type=F32))
        else:
            v = jnp.concatenate([vc_ref[:, h, :] for h in range(vc_ref.shape[1])],
                                axis=1).astype(BF16)
            update(s, lambda p: jnp.dot(p, v, preferred_element_type=F32))

    @pl.when(kstep == nk)
    def _():
        npad = LANES - tnew
        k = jnp.concatenate([kn_ref[...], jnp.zeros((npad, W), F32)], axis=0).astype(BF16)
        v = jnp.concatenate([vn_ref[...], jnp.zeros((npad, W), F32)], axis=0).astype(BF16)
        s = lax.dot_general(qbd_ref[...], k, _NT, preferred_element_type=F32)
        if mode == "fox":
            s = s + cq_ref[...] - expand_rows(ckn_ref[...])
        tpos = lax.broadcasted_iota(jnp.int32, (R, LANES), 0) % tnew
        spos = lax.broadcasted_iota(jnp.int32, (R, LANES), 1)
        if mode == "fox":
            visible = spos <= tpos
        else:
            visible = ((past + spos) // CHUNK) <= ((past + tpos) // CHUNK)
        visible = jnp.logical_and(visible, spos < tnew)
        update(jnp.where(visible, s, NEG_INF),
               lambda p: jnp.dot(p, v, preferred_element_type=F32))

        accn = acc_ref[...] / l_ref[...]
        if mode == "fox":
            lane_map = lax.broadcasted_iota(jnp.int32, (tnew, W), 1) // HEAD_DIM
            o = jnp.zeros((tnew, W), F32)
            for r in range(nmap):
                o = jnp.where(lane_map == r, accn[r * tnew:(r + 1) * tnew, :], o)
            o_ref[...] = o * jax.nn.sigmoid(gate_ref[...])
        else:
            lam = _diff_lambda(lam_ref, lam_init)
            for h in range(nmap // 2):
                sl = slice(h * LANES, (h + 1) * LANES)
                o = (accn[(2 * h) * tnew:(2 * h + 1) * tnew, sl]
                     - lam * accn[(2 * h + 1) * tnew:(2 * h + 2) * tnew, sl])
                y = o * lax.rsqrt(jnp.mean(o * o, axis=-1, keepdims=True) + EPS)
                o_ref[:, sl] = (y * subg_ref[...]) * (1.0 - lam_init)


def sample_attention(mode, out_buf, q, k, vsrc, v_colblk, kcache_t, vcache, v_time_minor,
                     layer_idx, nstream, tnew, row0, tk, *, cq=None, ck_cache=None, ck_new=None,
                     gate_src=None, gate_colblk=None, lam=None, sub_g=None, lam_init=0.0):
    W = q.shape[1]
    past = kcache_t.shape[2]
    nk = past // tk
    nmap = W // HEAD_DIM
    R = nmap * tnew
    rb0 = row0 // tnew
    wblk = lambda colblk: pl.BlockSpec((tnew, W), lambda b, s, c=colblk: (rb0 + b, c))
    time_minor = pl.BlockSpec((None, W, tk),
                              lambda b, s: (layer_idx * nstream + b, 0, jnp.minimum(s, nk - 1)))
    time_major = pl.BlockSpec((None, tk) + vcache.shape[2:],
                              lambda b, s: (layer_idx * nstream + b, jnp.minimum(s, nk - 1), 0, 0))
    in_specs = [wblk(0), time_minor, time_minor if v_time_minor else time_major,
                wblk(0), wblk(v_colblk)]
    args = [q, kcache_t, vcache, k, vsrc]
    if mode == "fox":
        in_specs += [pl.BlockSpec((None, R, 1), lambda b, s: (b, 0, 0)),
                     pl.BlockSpec((None, nmap, tk), lambda b, s: (b, 0, jnp.minimum(s, nk - 1))),
                     pl.BlockSpec((None, nmap, LANES), lambda b, s: (b, 0, 0)),
                     wblk(gate_colblk)]
        args += [cq, ck_cache, ck_new, gate_src]
    else:
        in_specs += [pl.BlockSpec(lam.shape, lambda b, s: (0, 0)),
                     pl.BlockSpec((1, LANES), lambda b, s: (0, 0))]
        args += [lam, sub_g]
    return pl.pallas_call(
        functools.partial(_sample_attn_kernel, mode=mode, lam_init=lam_init,
                          scale=1.0 / math.sqrt(HEAD_DIM), past=past, v_time_minor=v_time_minor),
        grid=(nstream, nk + 1),
        in_specs=[pl.BlockSpec(memory_space=pl.ANY)] + in_specs,
        out_specs=pl.BlockSpec((tnew, W), lambda b, s: (rb0 + b, 0)),
        out_shape=jax.ShapeDtypeStruct(out_buf.shape, F32),
        input_output_aliases={0: 0},
        scratch_shapes=[pltpu.VMEM((R, W), BF16), pltpu.VMEM((R, 1), F32),
                        pltpu.VMEM((R, 1), F32), pltpu.VMEM((R, W), F32)],
        compiler_params=_params(("arbitrary", "arbitrary")),
        name=f"sample_attention_{mode}",
    )(out_buf, *args)


def _top16_desc(e):
    rows = []
    for _ in range(PEER_TOPK):
        m = jnp.max(e, axis=0, keepdims=True)
        rows.append(m)
        e = jnp.where(e == m, -1.0, e)
    return jnp.concatenate(rows, axis=0)


def _compare_exchange(v, i, l):
    v[i], v[l] = jnp.maximum(v[i], v[l]), jnp.minimum(v[i], v[l])


def _top16_ranks(e):
    n = e.shape[0] // SUBLANES
    assert n == PEER_TOPK
    v = [e[r * SUBLANES:(r + 1) * SUBLANES, :] for r in range(n)]
    k = 2
    while k <= n:
        j = k // 2
        while j >= 1:
            for i in range(n):
                l = i ^ j
                if l > i:
                    if i & k == 0:
                        _compare_exchange(v, i, l)
                    else:
                        _compare_exchange(v, l, i)
            j //= 2
        k *= 2
    shift = SUBLANES // 2
    while shift >= 1:
        w = [pltpu.roll(x, shift, axis=0) for x in v]
        v = [jnp.maximum(v[i], w[n - 1 - i]) for i in range(n)]
        j = n // 2
        while j >= 1:
            for i in range(n):
                l = i ^ j
                if l > i:
                    _compare_exchange(v, i, l)
            j //= 2
        shift //= 2
    return v


def _ranks_on_sublanes(v, first):
    row = lax.broadcasted_iota(jnp.int32, v[0].shape, 0)
    out = v[first + SUBLANES - 1]
    for r in range(SUBLANES - 2, -1, -1):
        out = jnp.where(row == r, v[first + r], out)
    return out


def _pair_candidates(a, b):
    row = lax.broadcasted_iota(jnp.int32, a[0].shape, 0)
    b_lo = _ranks_on_sublanes(b, 0)
    b_hi = _ranks_on_sublanes(b, SUBLANES)
    parts = [b_lo * a[0], b_hi * a[0], b_lo * a[1]]
    for r1 in range(2, SUBLANES):
        parts.append(jnp.where(row < PEER_TOPK // (r1 + 1), b_lo * a[r1], -1.0))
    parts.append(b[0] * _ranks_on_sublanes(a, SUBLANES))
    return jnp.concatenate(parts, axis=0)


def _peer_kernel(x_ref, g_ref, wq_ref, sk_ref, u0_ref, un_ref, vp_ref, vl_ref, o_ref,
                 hn_ref, e1_ref, e2_ref, thr_ref, at0_ref, at1_ref, wt0_ref, wt1_ref, yt_ref,
                 *, nheads):
    tb, D = x_ref.shape
    te = un_ref.shape[0]
    nkeys = sk_ref.shape[1]
    half = sk_ref.shape[2]
    e_step = pl.program_id(1)
    ne = pl.num_programs(1)
    assert te == SUBLANES * nkeys
    chunk_w = min(MXU_WIDTH, tb)
    nchunk = tb // chunk_w

    @pl.when(e_step == 0)
    def _():
        x = x_ref[...]
        y = x * lax.rsqrt(jnp.mean(x * x, axis=-1, keepdims=True) + EPS)
        hn_ref[...] = (y * g_ref[...]).astype(BF16)

        def head_body(h, carry):
            es = []
            for c in range(2):
                r0 = pl.multiple_of((h * 2 + c) * half, half)
                qs = lax.dot_general(wq_ref[pl.ds(r0, half), :], hn_ref[...], _NT,
                                     preferred_element_type=F32).astype(BF16)
                s = jnp.dot(sk_ref[h * 2 + c], qs, preferred_element_type=F32)
                es.append(jnp.exp(s - jnp.max(s, axis=0, keepdims=True)))
            e2_ref[h] = es[1]
            tiny = jnp.finfo(F32).tiny
            for tc in range(tb // LANES):
                sl = slice(tc * LANES, (tc + 1) * LANES)
                e1 = es[0][:, sl]
                a = _top16_ranks(e1)
                b = _top16_ranks(es[1][:, sl])
                cand = _pair_candidates(a, b)
                top = jnp.maximum(_top16_desc(cand), 0.0)
                rz = 1.0 / jnp.sum(top, axis=0, keepdims=True)
                winner = cand >= jnp.maximum(top[PEER_TOPK - 1:PEER_TOPK, :], tiny)
                cand_n = _pair_candidates([ar * rz for ar in a], b)
                thr = jnp.min(jnp.where(winner, cand_n, 2.0), axis=0, keepdims=True)
                e1_ref[h, :, sl] = e1 * rz
                thr_ref[h, :, sl] = jnp.broadcast_to(jnp.maximum(thr, tiny), (SUBLANES, LANES))
            return carry

        lax.fori_loop(0, nheads, head_body, 0)
        at0_ref[...] = lax.dot_general(u0_ref[...], hn_ref[...], _NT, preferred_element_type=F32)
        wt1_ref[...] = jnp.zeros(wt1_ref.shape, BF16)
        yt_ref[...] = jnp.zeros(yt_ref.shape, F32)

    pieces = ([("pre", r, c) for c in range(nchunk) for r in range(te // PIECE_ROWS)]
              + [("out", r, c) for c in range(nchunk) for r in range(D // PIECE_ROWS)])

    def mxu_piece(piece, at_nxt, wt_prv):
        kind, r, c = piece
        rows = slice(r * PIECE_ROWS, (r + 1) * PIECE_ROWS)
        cols = slice(c * chunk_w, (c + 1) * chunk_w)
        if kind == "pre":
            at_nxt[rows, cols] = lax.dot_general(un_ref[rows, :], hn_ref[cols, :], _NT,
                                                 preferred_element_type=F32)
        else:
            yt_ref[rows, cols] += jnp.dot(vp_ref[rows, :], wt_prv[:, cols],
                                          preferred_element_type=F32)

    def gate_unit(tc, ii, e1_tiles, at_cur, wt_cur):
        sl = slice(tc * LANES, (tc + 1) * LANES)
        for jh in range(nkeys // GATE_ROWS):
            js = slice(jh * GATE_ROWS, (jh + 1) * GATE_ROWS)
            gate = jnp.zeros((GATE_ROWS, LANES), F32)
            for h in range(nheads):
                t = e2_ref[h, js, sl] * e1_tiles[h][ii:ii + 1, :]
                gate = gate + jnp.where(t >= thr_ref[h, 0:1, sl], t, 0.0)
            rows = slice(ii * nkeys + jh * GATE_ROWS, ii * nkeys + (jh + 1) * GATE_ROWS)
            a = at_cur[rows, sl]
            act = 0.5 * a * (1.0 + lax.erf(a * (1.0 / math.sqrt(2.0))))
            wt_cur[rows, sl] = (gate * act).astype(BF16)

    def body(at_cur, at_nxt, wt_cur, wt_prv):
        i0 = pl.multiple_of(e_step * SUBLANES, SUBLANES)
        n_units = (tb // LANES) * SUBLANES
        done = 0
        for tc in range(tb // LANES):
            sl = slice(tc * LANES, (tc + 1) * LANES)
            e1_tiles = [e1_ref[h, pl.ds(i0, SUBLANES), sl] for h in range(nheads)]
            for ii in range(SUBLANES):
                unit = tc * SUBLANES + ii
                upto = (unit + 1) * len(pieces) // n_units
                for piece in pieces[done:upto]:
                    mxu_piece(piece, at_nxt, wt_prv)
                done = upto
                gate_unit(tc, ii, e1_tiles, at_cur, wt_cur)

    @pl.when(e_step % 2 == 0)
    def _():
        body(at0_ref, at1_ref, wt0_ref, wt1_ref)

    @pl.when(e_step % 2 == 1)
    def _():
        body(at1_ref, at0_ref, wt1_ref, wt0_ref)

    @pl.when(e_step == ne - 1)
    def _():
        yt = yt_ref[...] + jnp.dot(vl_ref[...], wt1_ref[...], preferred_element_type=F32)
        o_ref[...] = x_ref[...] + yt.T


def peer_layer(x, g, layer, wq_t, sub_keys, u, v_t, tb):
    M, D = x.shape
    E = u.shape[1]
    nheads = sub_keys.shape[1] // 2
    nkeys = sub_keys.shape[2]
    te = SUBLANES * nkeys
    ne = E // te
    assert ne % 2 == 0
    return pl.pallas_call(
        functools.partial(_peer_kernel, nheads=nheads),
        grid=(M // tb, ne),
        in_specs=[pl.BlockSpec((tb, D), lambda i, e: (i, 0)),
                  pl.BlockSpec((1, D), lambda i, e: (0, 0)),
                  pl.BlockSpec((None,) + wq_t.shape[1:], lambda i, e: (layer, 0, 0)),
                  pl.BlockSpec((None,) + sub_keys.shape[1:], lambda i, e: (layer, 0, 0, 0)),
                  pl.BlockSpec((None, te, D), lambda i, e: (layer, 0, 0)),
                  pl.BlockSpec((None, te, D),
                               lambda i, e: (layer, jnp.minimum(e + 1, ne - 1), 0)),
                  pl.BlockSpec((None, D, te), lambda i, e: (layer, 0, jnp.maximum(e - 1, 0))),
                  pl.BlockSpec((None, D, te), lambda i, e: (layer, 0, ne - 1))],
        out_specs=pl.BlockSpec((tb, D), lambda i, e: (i, 0)),
        out_shape=jax.ShapeDtypeStruct((M, D), F32),
        scratch_shapes=[pltpu.VMEM((tb, D), BF16),
                        pltpu.VMEM((nheads, nkeys, tb), F32),
                        pltpu.VMEM((nheads, nkeys, tb), F32),
                        pltpu.VMEM((nheads, SUBLANES, tb), F32),
                        pltpu.VMEM((te, tb), F32),
                        pltpu.VMEM((te, tb), F32),
                        pltpu.VMEM((te, tb), BF16),
                        pltpu.VMEM((te, tb), BF16),
                        pltpu.VMEM((D, tb), F32)],
        compiler_params=_params(("arbitrary", "arbitrary")),
        name="peer",
    )(x, g, wq_t, sub_keys, u, u, v_t, v_t)


def _rope_tables(pos):
    inv = ROPE_THETA ** (-jnp.arange(0, HEAD_DIM, 2, dtype=F32) / HEAD_DIM)
    ang = pos.astype(F32)[:, None] * inv[None, :]
    ang = jnp.concatenate([ang, ang, ang, ang], axis=-1)
    return jnp.cos(ang), jnp.sin(ang)


def _tile_heads(g, width):
    return jnp.tile(g, width // g.shape[0]).reshape(1, width)


def _time_minor(cache):
    nd = cache.ndim
    c = jnp.transpose(cache, (0, 1) + tuple(range(3, nd)) + (2,))
    return c.reshape(c.shape[0] * c.shape[1], -1, c.shape[-1])


def _forward(x_prompt, x_sample, cache_fox_k, cache_fox_v, cache_fox_logf, cache_diff_k,
             cache_diff_v, norm_mix_g, norm_ffn_g, fox_w_in, fox_b_f, fox_q_g, fox_k_g, fox_w_o,
             diff_w_in, diff_q_g, diff_k_g, diff_lambda, diff_sub_g, diff_w_o,
             peer_w_q, peer_sub_keys, peer_u, peer_v, *, tm, tq, tk_cache, tb):
    B, S, D = x_prompt.shape
    NS, T, _ = x_sample.shape
    depth = norm_mix_g.shape[0]
    past = cache_fox_k.shape[2]
    MP = B * S
    M = MP + NS * T
    fox_heads = cache_fox_k.shape[3]
    diff_heads = cache_diff_k.shape[3]
    W = fox_heads * HEAD_DIM
    npair = W // LANES

    x = jnp.concatenate([x_prompt.reshape(MP, D), x_sample.reshape(NS * T, D)], axis=0)

    pos = jnp.concatenate([jnp.tile(jnp.arange(S, dtype=jnp.int32), B),
                           jnp.tile(past + jnp.arange(T, dtype=jnp.int32), NS)])
    cos, sin = _rope_tables(pos)

    kc_fox = _time_minor(cache_fox_k)
    vc_fox = _time_minor(cache_fox_v)
    kc_diff = _time_minor(cache_diff_k)
    vc_diff = cache_diff_v.reshape((-1, past) + cache_diff_v.shape[3:])
    lf_cache = jnp.swapaxes(cache_fox_logf, 2, 3)

    wq_t = jnp.swapaxes(peer_w_q, 1, 2).astype(BF16)
    sk_all = peer_sub_keys.reshape((depth, -1) + peer_sub_keys.shape[3:]).astype(BF16)
    u_all = peer_u.astype(BF16)
    v_t_all = jnp.swapaxes(peer_v, 1, 2).astype(BF16)

    n_fox, n_diff = cache_fox_k.shape[0], cache_diff_k.shape[0]
    outs = {k: [] for k in ("fl_p", "fk_s", "fv_s", "fl_s", "dk_s", "dv_s")}
    fk_buf = fv_buf = dk_buf = dv_buf = None

    for layer in range(depth):
        j = layer // 2
        g_mix = norm_mix_g[layer].reshape(1, D)
        if layer % 2 == 0:
            w_in = fox_w_in[j]
            w_main = w_in[:, :4 * W].astype(BF16)
            w_f = jnp.pad(w_in[:, 4 * W:], ((0, 0), (0, LANES - fox_heads))).astype(BF16)
            p = norm_matmul(x, g_mix, w_main, tm, 1024)
            pf = norm_matmul(x, g_mix, w_f, tm, LANES)
            b_f = jnp.pad(fox_b_f[j], (0, LANES - fox_heads)).reshape(1, LANES)
            qn, kn, lf, lf_t = fox_post(p, pf, b_f, _tile_heads(fox_q_g[j], W),
                                        _tile_heads(fox_k_g[j], W), tm)
            lf = lf[:, :fox_heads]
            lf_p = lf[:MP].reshape(B, S, fox_heads)
            lf_s = lf[MP:].reshape(NS, T, fox_heads)
            c_t = cumsum_lanes(lf_t[:, :MP], LANES, S)[:fox_heads]
            c_pt = jnp.swapaxes(c_t.reshape(fox_heads, B, S), 0, 1)
            c_rows = (c_pt * LOG2E).reshape(B, npair, 2, S)
            c_cols = jnp.swapaxes(c_rows, 2, 3)
            o_p = prompt_attention("fox", qn, kn, p, 2 * npair, B, S, tq,
                                   c_rows=c_rows, c_cols=c_cols, gate_src=p,
                                   gate_col0=3 * npair)
            c_cache = cumsum_lanes(lf_cache[j].reshape(NS * fox_heads, past), LANES, past)
            lf_new = jnp.pad(jnp.swapaxes(lf_s, 1, 2), ((0, 0), (0, 0), (0, LANES - T)))
            c_new = cumsum_lanes(lf_new.reshape(NS * fox_heads, LANES), LANES, LANES)
            ck_new = (c_cache[:, past - 1:] + c_new).reshape(NS, fox_heads, LANES)
            cq_s = ck_new[:, :, :T].reshape(NS, fox_heads * T, 1)
            o = sample_attention("fox", o_p, qn, kn, p, 2, kc_fox, vc_fox, True, j, NS, T, MP,
                                 tk_cache, cq=cq_s,
                                 ck_cache=c_cache.reshape(NS, fox_heads, past), ck_new=ck_new,
                                 gate_src=p, gate_colblk=3)
            x = matmul_residual(o, fox_w_o[j].astype(BF16), x, tm)
            fk_buf = emit_prompt_rows(kn, 0, W, j, n_fox, B, S, tm, True, fk_buf)
            fv_buf = emit_prompt_rows(p, 2, W, j, n_fox, B, S, tm, True, fv_buf)
            outs["fl_p"].append(lf_p)
            outs["fk_s"].append(kn[MP:].reshape(NS, T, fox_heads, HEAD_DIM))
            outs["fv_s"].append(p[MP:, 2 * W:3 * W].reshape(NS, T, fox_heads, HEAD_DIM))
            outs["fl_s"].append(lf_s)
        else:
            lam_init = 0.8 - 0.6 * math.exp(-0.3 * layer)
            p = norm_matmul(x, g_mix, diff_w_in[j].astype(BF16), tm, 1024)
            qn, kn = diff_post(p, cos, sin, _tile_heads(diff_q_g[j], W),
                               _tile_heads(diff_k_g[j], W), tm)
            sub_g = diff_sub_g[j].reshape(1, LANES)
            o_p = prompt_attention("diff", qn, kn, p, 2 * npair, B, S, tq,
                                   lam=diff_lambda[j], sub_g=sub_g, lam_init=lam_init)
            o = sample_attention("diff", o_p, qn, kn, p, 2, kc_diff, vc_diff, False, j, NS, T,
                                 MP, tk_cache, lam=diff_lambda[j], sub_g=sub_g,
                                 lam_init=lam_init)
            x = matmul_residual(o, diff_w_o[j].astype(BF16), x, tm)
            dk_buf = emit_prompt_rows(kn, 0, W, j, n_diff, B, S, tm, True, dk_buf)
            dv_buf = emit_prompt_rows(p, 2, W, j, n_diff, B, S, tm, False, dv_buf)
            outs["dk_s"].append(kn[MP:].reshape(NS, T, diff_heads, 2, HEAD_DIM))
            outs["dv_s"].append(p[MP:, 2 * W:3 * W].reshape(NS, T, diff_heads, 2 * HEAD_DIM))

        x = peer_layer(x, norm_ffn_g[layer].reshape(1, D), layer, wq_t, sk_all, u_all, v_t_all,
                       tb)

    fk_p = jnp.transpose(fk_buf.reshape(n_fox, B, fox_heads, HEAD_DIM, S), (0, 1, 4, 2, 3))
    fv_p = jnp.transpose(fv_buf.reshape(n_fox, B, fox_heads, HEAD_DIM, S), (0, 1, 4, 2, 3))
    dk_p = jnp.transpose(dk_buf.reshape(n_diff, B, diff_heads, 2, HEAD_DIM, S),
                         (0, 1, 5, 2, 3, 4))
    dv_p = dv_buf.reshape(n_diff, B, S, diff_heads, 2 * HEAD_DIM)
    return (x[:MP].reshape(B, S, D), x[MP:].reshape(NS, T, D),
            fk_p, fv_p, jnp.stack(outs["fl_p"]), dk_p, dv_p,
            jnp.stack(outs["fk_s"]), jnp.stack(outs["fv_s"]), jnp.stack(outs["fl_s"]),
            jnp.stack(outs["dk_s"]), jnp.stack(outs["dv_s"]))


def kernel(x_prompt, x_sample, cache_fox_k, cache_fox_v, cache_fox_logf, cache_diff_k, cache_diff_v, norm_mix_g, norm_ffn_g, fox_w_in, fox_b_f, fox_q_g, fox_k_g, fox_w_o, diff_w_in, diff_q_g, diff_k_g, diff_lambda, diff_sub_g, diff_w_o, peer_w_q, peer_sub_keys, peer_u, peer_v):
    return _forward(x_prompt, x_sample, cache_fox_k, cache_fox_v, cache_fox_logf, cache_diff_k,
                    cache_diff_v, norm_mix_g, norm_ffn_g, fox_w_in, fox_b_f, fox_q_g, fox_k_g,
                    fox_w_o, diff_w_in, diff_q_g, diff_k_g, diff_lambda, diff_sub_g, diff_w_o,
                    peer_w_q, peer_sub_keys, peer_u, peer_v,
                    tm=512, tq=512, tk_cache=512, tb=512)
```

```python
import functools
import math

import jax
import jax.numpy as jnp
from jax import lax
from jax.experimental import pallas as pl
from jax.experimental.pallas import tpu as pltpu

F32 = jnp.float32
BF16 = jnp.bfloat16
EPS = 1e-6
NEG_INF = -1e30
CHUNK = 64
ROPE_THETA = 10000.0
LOG2E = math.log2(math.e)
PEER_TOPK = 16
LANES = 128
SUBLANES = 8
MXU_WIDTH = 256
GATE_ROWS = 32
PIECE_ROWS = 1024
HEAD_DIM = 64
VMEM_LIMIT = 56 * 1024 * 1024

_NT = (((1,), (1,)), ((), ()))


def _params(sem, vmem=VMEM_LIMIT):
    return pltpu.CompilerParams(dimension_semantics=sem, vmem_limit_bytes=vmem)


def _norm_matmul_kernel(x_ref, g_ref, w_ref, o_ref, hn_ref):
    @pl.when(pl.program_id(1) == 0)
    def _():
        x = x_ref[...]
        y = x * lax.rsqrt(jnp.mean(x * x, axis=-1, keepdims=True) + EPS)
        hn_ref[...] = (y * g_ref[...]).astype(BF16)

    o_ref[...] = jnp.dot(hn_ref[...], w_ref[...], preferred_element_type=F32)


def norm_matmul(x, g, w, tm, tn):
    M, D = x.shape
    N = w.shape[1]
    return pl.pallas_call(
        _norm_matmul_kernel,
        grid=(M // tm, N // tn),
        in_specs=[pl.BlockSpec((tm, D), lambda i, j: (i, 0)),
                  pl.BlockSpec((1, D), lambda i, j: (0, 0)),
                  pl.BlockSpec((D, tn), lambda i, j: (0, j))],
        out_specs=pl.BlockSpec((tm, tn), lambda i, j: (i, j)),
        out_shape=jax.ShapeDtypeStruct((M, N), F32),
        scratch_shapes=[pltpu.VMEM((tm, D), BF16)],
        compiler_params=_params(("arbitrary", "arbitrary")),
        name="norm_matmul",
    )(x, g, w)


def _matmul_res_kernel(a_ref, w_ref, r_ref, o_ref):
    o_ref[...] = r_ref[...] + jnp.dot(a_ref[...].astype(BF16), w_ref[...],
                                      preferred_element_type=F32)


def matmul_residual(a, w, res, tm):
    M, K = a.shape
    N = w.shape[1]
    return pl.pallas_call(
        _matmul_res_kernel,
        grid=(M // tm,),
        in_specs=[pl.BlockSpec((tm, K), lambda i: (i, 0)),
                  pl.BlockSpec((K, N), lambda i: (0, 0)),
                  pl.BlockSpec((tm, N), lambda i: (i, 0))],
        out_specs=pl.BlockSpec((tm, N), lambda i: (i, 0)),
        out_shape=jax.ShapeDtypeStruct((M, N), F32),
        compiler_params=_params(("arbitrary",)),
        name="matmul_residual",
    )(a, w, res)


def _emit_kernel(src_ref, *rest, transpose):
    o_ref = rest[-1]
    o_ref[...] = src_ref[...].T if transpose else src_ref[...]


def emit_prompt_rows(src, colblk, width, layer, nlayers, batch, seq, tm, transpose, buf=None):
    nblk = seq // tm
    in_specs = [pl.BlockSpec((tm, width), lambda b, s: (b * nblk + s, colblk))]
    args = [src]
    if buf is not None:
        in_specs.append(pl.BlockSpec(memory_space=pl.ANY))
        args.append(buf)
    if transpose:
        out_shape = (nlayers, batch, width, seq)
        out_spec = pl.BlockSpec((None, None, width, tm), lambda b, s: (layer, b, 0, s))
    else:
        out_shape = (nlayers, batch, seq, width)
        out_spec = pl.BlockSpec((None, None, tm, width), lambda b, s: (layer, b, s, 0))
    return pl.pallas_call(
        functools.partial(_emit_kernel, transpose=transpose),
        grid=(batch, nblk),
        in_specs=in_specs,
        out_specs=out_spec,
        out_shape=jax.ShapeDtypeStruct(out_shape, F32),
        input_output_aliases={} if buf is None else {1: 0},
        compiler_params=_params(("arbitrary", "arbitrary")),
        name="emit_prompt_rows",
    )(*args)


def _head_rms64(xc, lo):
    sq = xc * xc
    s_lo = jnp.sum(jnp.where(lo, sq, 0.0), axis=-1, keepdims=True)
    s_hi = jnp.sum(jnp.where(lo, 0.0, sq), axis=-1, keepdims=True)
    ms = jnp.where(lo, s_lo, s_hi) * (1.0 / HEAD_DIM)
    return lax.rsqrt(ms + EPS)


def _fox_post_kernel(q_ref, k_ref, f_ref, bf_ref, qg_ref, kg_ref, qo_ref, ko_ref, lf_ref,
                     lft_ref):
    tm, width = q_ref.shape
    lo = lax.broadcasted_iota(jnp.int32, (tm, LANES), 1) < HEAD_DIM
    for src, g_ref, dst in ((q_ref, qg_ref, qo_ref), (k_ref, kg_ref, ko_ref)):
        for c in range(width // LANES):
            sl = slice(c * LANES, (c + 1) * LANES)
            xc = src[:, sl]
            dst[:, sl] = xc * _head_rms64(xc, lo) * g_ref[:, sl]
    z = f_ref[...] + bf_ref[...]
    lf = jnp.minimum(z, 0.0) - jnp.log1p(jnp.exp(-jnp.abs(z)))
    lf_ref[...] = lf
    lft_ref[...] = lf.T


def fox_post(p, b_f, q_g, k_g, tm):
    M = p.shape[0]
    W = q_g.shape[1]
    row = lambda c: pl.BlockSpec((tm, W), lambda i, c=c: (i, c))
    vec = lambda n: pl.BlockSpec((1, n), lambda i: (0, 0))
    return pl.pallas_call(
        _fox_post_kernel,
        grid=(M // tm,),
        in_specs=[row(0), row(1), pl.BlockSpec((tm, LANES), lambda i: (i, 4 * W // LANES)),
                  vec(LANES), vec(W), vec(W)],
        out_specs=[row(0), row(0), pl.BlockSpec((tm, LANES), lambda i: (i, 0)),
                   pl.BlockSpec((LANES, tm), lambda i: (0, i))],
        out_shape=[jax.ShapeDtypeStruct((M, W), F32), jax.ShapeDtypeStruct((M, W), F32),
                   jax.ShapeDtypeStruct((M, LANES), F32), jax.ShapeDtypeStruct((LANES, M), F32)],
        compiler_params=_params(("arbitrary",)),
        name="fox_post",
    )(p, p, p, b_f, q_g, k_g)


def _diff_post_kernel(q_ref, k_ref, cos_ref, sin_ref, qg_ref, kg_ref, qo_ref, ko_ref):
    tm, width = q_ref.shape
    lane = lax.broadcasted_iota(jnp.int32, (tm, LANES), 1)
    lo = lane < HEAD_DIM
    first_half = (lane % HEAD_DIM) < (HEAD_DIM // 2)
    cos = cos_ref[...]
    sin = sin_ref[...]
    for src, g_ref, dst in ((q_ref, qg_ref, qo_ref), (k_ref, kg_ref, ko_ref)):
        for c in range(width // LANES):
            sl = slice(c * LANES, (c + 1) * LANES)
            xc = src[:, sl]
            xn = xc * _head_rms64(xc, lo) * g_ref[:, sl]
            rot = jnp.where(first_half,
                            -pltpu.roll(xn, LANES - HEAD_DIM // 2, axis=1),
                            pltpu.roll(xn, HEAD_DIM // 2, axis=1))
            dst[:, sl] = xn * cos + rot * sin


def diff_post(p, cos, sin, q_g, k_g, tm):
    M = p.shape[0]
    W = q_g.shape[1]
    row = lambda c: pl.BlockSpec((tm, W), lambda i, c=c: (i, c))
    tab = pl.BlockSpec((tm, LANES), lambda i: (i, 0))
    vec = pl.BlockSpec((1, W), lambda i: (0, 0))
    return pl.pallas_call(
        _diff_post_kernel,
        grid=(M // tm,),
        in_specs=[row(0), row(1), tab, tab, vec, vec],
        out_specs=[row(0), row(0)],
        out_shape=[jax.ShapeDtypeStruct((M, W), F32), jax.ShapeDtypeStruct((M, W), F32)],
        compiler_params=_params(("arbitrary",)),
        name="diff_post",
    )(p, p, cos, sin, q_g, k_g)


def _cumsum_kernel(x_ref, o_ref):
    rows, length = x_ref.shape
    s_idx = lax.broadcasted_iota(jnp.int32, (LANES, LANES), 0)
    t_idx = lax.broadcasted_iota(jnp.int32, (LANES, LANES), 1)
    tri = (s_idx <= t_idx).astype(F32)
    carry = jnp.zeros((rows, 1), F32)
    for blk in range(length // LANES):
        sl = slice(blk * LANES, (blk + 1) * LANES)
        c = jnp.dot(x_ref[:, sl], tri, preferred_element_type=F32,
                    precision=lax.Precision.HIGHEST) + carry
        o_ref[:, sl] = c
        carry = c[:, LANES - 1:LANES]


def cumsum_lanes(x, rows, length):
    R, L = x.shape
    return pl.pallas_call(
        _cumsum_kernel,
        grid=(R // rows, L // length),
        in_specs=[pl.BlockSpec((rows, length), lambda i, j: (i, j))],
        out_specs=pl.BlockSpec((rows, length), lambda i, j: (i, j)),
        out_shape=jax.ShapeDtypeStruct((R, L), F32),
        compiler_params=_params(("arbitrary", "arbitrary")),
        name="cumsum_lanes",
    )(x)


def _diff_lambda(lam_ref, lam_init):
    lf = lam_ref[...]
    a = jnp.sum(lf[0:1, :] * lf[1:2, :], axis=-1, keepdims=True)
    b = jnp.sum(lf[2:3, :] * lf[3:4, :], axis=-1, keepdims=True)
    return jnp.exp(a) - jnp.exp(b) + lam_init


def _prompt_attn_kernel(qi_tab, ki_tab, q_ref, k_ref, v_ref, *rest, mode, lam_init, scale):
    if mode == "fox":
        cq_ref, ck_ref, gate_ref, o_ref = rest[:4]
        scratch = rest[4:]
    else:
        lam_ref, subg_ref, o_ref = rest[:3]
        scratch = rest[3:]
    qs_ref, m0_ref, m1_ref, l0_ref, l1_ref, acc0_ref, acc1_ref = scratch
    tq = q_ref.shape[0]
    tk = k_ref.shape[0]
    t = pl.program_id(2)
    qi = qi_tab[t]
    ki = ki_tab[t]

    @pl.when(ki == 0)
    def _():
        qs_ref[...] = (q_ref[...] * scale).astype(BF16)
        for m_ref, l_ref, acc_ref in ((m0_ref, l0_ref, acc0_ref), (m1_ref, l1_ref, acc1_ref)):
            m_ref[...] = jnp.full(m_ref.shape, NEG_INF, F32)
            l_ref[...] = jnp.zeros(l_ref.shape, F32)
            acc_ref[...] = jnp.zeros(acc_ref.shape, F32)

    def update(s, vt, m_ref, l_ref, acc_ref):
        m_prev = m_ref[...]
        m_new = jnp.maximum(m_prev, jnp.max(s, axis=0, keepdims=True))
        alpha = jnp.exp2(m_prev - m_new)
        p = jnp.exp2(s - m_new)
        l_ref[...] = alpha * l_ref[...] + jnp.sum(p, axis=0, keepdims=True)
        acc_ref[...] = alpha * acc_ref[...] + jnp.dot(vt, p.astype(BF16),
                                                      preferred_element_type=F32)
        m_ref[...] = m_new

    def step(diagonal):
        k = k_ref[...]
        lo_k = lax.broadcasted_iota(jnp.int32, (tk, LANES), 1) < HEAD_DIM
        k0 = jnp.where(lo_k, k, 0.0).astype(BF16)
        k1 = jnp.where(lo_k, 0.0, k).astype(BF16)
        s0 = lax.dot_general(k0, qs_ref[...], _NT, preferred_element_type=F32)
        s1 = lax.dot_general(k1, qs_ref[...], _NT, preferred_element_type=F32)
        if mode == "fox":
            s0 = s0 + cq_ref[0:1, :] - ck_ref[:, 0:1]
            s1 = s1 + cq_ref[1:2, :] - ck_ref[:, 1:2]
        if diagonal:
            key = lax.broadcasted_iota(jnp.int32, (tk, tq), 0)
            qry = lax.broadcasted_iota(jnp.int32, (tk, tq), 1)
            if mode == "fox":
                visible = key <= qry
            else:
                visible = (key // CHUNK) <= (qry // CHUNK)
            s0 = jnp.where(visible, s0, NEG_INF)
            s1 = jnp.where(visible, s1, NEG_INF)
        vt = v_ref[...].T.astype(BF16)
        update(s0, vt, m0_ref, l0_ref, acc0_ref)
        update(s1, vt, m1_ref, l1_ref, acc1_ref)

    @pl.when(ki < qi)
    def _():
        step(False)

    @pl.when(ki == qi)
    def _():
        step(True)
        o0 = acc0_ref[...] / l0_ref[...]
        o1 = acc1_ref[...] / l1_ref[...]
        if mode == "fox":
            first_head = lax.broadcasted_iota(jnp.int32, (LANES, tq), 0) < HEAD_DIM
            ot = jnp.where(first_head, o0, o1)
            o_ref[...] = ot.T * jax.nn.sigmoid(gate_ref[...])
        else:
            ot = o0 - _diff_lambda(lam_ref, lam_init) * o1
            y = ot * lax.rsqrt(jnp.mean(ot * ot, axis=0, keepdims=True) + EPS)
            o_ref[...] = (y.T * subg_ref[...]) * (1.0 - lam_init)


def prompt_attention(mode, q, k, vsrc, v_col0, batch, seq, tq, *, c_rows=None, c_cols=None,
                     gate_src=None, gate_col0=None, lam=None, sub_g=None, lam_init=0.0):
    W = q.shape[1]
    npair = W // LANES
    nq = seq // tq
    tri = [(a, b) for a in range(nq) for b in range(a + 1)]
    qi_tab = jnp.asarray([a for a, _ in tri], jnp.int32)
    ki_tab = jnp.asarray([b for _, b in tri], jnp.int32)
    qmap = lambda b, h, t, qt, kt: (b * nq + qt[t], h)
    kmap = lambda b, h, t, qt, kt: (b * nq + kt[t], h)
    in_specs = [pl.BlockSpec((tq, LANES), qmap),
                pl.BlockSpec((tq, LANES), kmap),
                pl.BlockSpec((tq, LANES), lambda b, h, t, qt, kt: (b * nq + kt[t], v_col0 + h))]
    args = [q, k, vsrc]
    if mode == "fox":
        in_specs += [pl.BlockSpec((None, None, 2, tq), lambda b, h, t, qt, kt: (b, h, 0, qt[t])),
                     pl.BlockSpec((None, None, tq, 2), lambda b, h, t, qt, kt: (b, h, kt[t], 0)),
                     pl.BlockSpec((tq, LANES),
                                  lambda b, h, t, qt, kt: (b * nq + qt[t], gate_col0 + h))]
        args += [c_rows, c_cols, gate_src]
    else:
        in_specs += [pl.BlockSpec(lam.shape, lambda b, h, t, qt, kt: (0, 0)),
                     pl.BlockSpec((1, LANES), lambda b, h, t, qt, kt: (0, 0))]
        args += [lam, sub_g]
    stat = pltpu.VMEM((1, tq), F32)
    acc = pltpu.VMEM((LANES, tq), F32)
    grid_spec = pltpu.PrefetchScalarGridSpec(
        num_scalar_prefetch=2,
        grid=(batch, npair, len(tri)),
        in_specs=in_specs,
        out_specs=pl.BlockSpec((tq, LANES), qmap),
        scratch_shapes=[pltpu.VMEM((tq, LANES), BF16), stat, stat, stat, stat, acc, acc],
    )
    return pl.pallas_call(
        functools.partial(_prompt_attn_kernel, mode=mode, lam_init=lam_init,
                          scale=LOG2E / math.sqrt(HEAD_DIM)),
        grid_spec=grid_spec,
        out_shape=jax.ShapeDtypeStruct(q.shape, F32),
        compiler_params=_params(("arbitrary", "arbitrary", "arbitrary")),
        name=f"prompt_attention_{mode}",
    )(qi_tab, ki_tab, *args)


def _sample_attn_kernel(buf_ref, q_ref, kc_ref, vc_ref, kn_ref, vn_ref, *rest, mode, lam_init,
                        scale, past, v_time_minor):
    del buf_ref
    if mode == "fox":
        cq_ref, ckc_ref, ckn_ref, gate_ref, o_ref = rest[:5]
        scratch = rest[5:]
    else:
        lam_ref, subg_ref, o_ref = rest[:3]
        scratch = rest[3:]
    qbd_ref, m_ref, l_ref, acc_ref = scratch
    tnew, W = q_ref.shape
    nmap = W // HEAD_DIM
    R = nmap * tnew
    kstep = pl.program_id(1)
    nk = pl.num_programs(1) - 1

    @pl.when(kstep == 0)
    def _():
        q = q_ref[...] * scale
        qt = jnp.concatenate([q] * nmap, axis=0)
        rmap = lax.broadcasted_iota(jnp.int32, (R, W), 0) // tnew
        cmap = lax.broadcasted_iota(jnp.int32, (R, W), 1) // HEAD_DIM
        qbd_ref[...] = jnp.where(rmap == cmap, qt, 0.0).astype(BF16)
        m_ref[...] = jnp.full(m_ref.shape, NEG_INF, F32)
        l_ref[...] = jnp.zeros(l_ref.shape, F32)
        acc_ref[...] = jnp.zeros(acc_ref.shape, F32)

    def expand_rows(c):
        n = c.shape[1]
        return jnp.broadcast_to(c[:, None, :], (nmap, tnew, n)).reshape(R, n)

    def update(s, pv):
        m_prev = m_ref[...]
        m_new = jnp.maximum(m_prev, jnp.max(s, axis=-1, keepdims=True))
        alpha = jnp.exp(m_prev - m_new)
        p = jnp.exp(s - m_new)
        l_ref[...] = alpha * l_ref[...] + jnp.sum(p, axis=-1, keepdims=True)
        acc_ref[...] = alpha * acc_ref[...] + pv(p.astype(BF16))
        m_ref[...] = m_new

    @pl.when(kstep < nk)
    def _():
        s = jnp.dot(qbd_ref[...], kc_ref[...].astype(BF16), preferred_element_type=F32)
        if mode == "fox":
            s = s + cq_ref[...] - expand_rows(ckc_ref[...])
        if v_time_minor:
            v = vc_ref[...].astype(BF16)
            update(s, lambda p: lax.dot_general(p, v, _NT, preferred_element_type=F32))
        else:
            v = jnp.concatenate([vc_ref[:, h, :] for h in range(vc_ref.shape[1])],
                                axis=1).astype(BF16)
            update(s, lambda p: jnp.dot(p, v, preferred_element_type=F32))

    @pl.when(kstep == nk)
    def _():
        npad = LANES - tnew
        k = jnp.concatenate([kn_ref[...], jnp.zeros((npad, W), F32)], axis=0).astype(BF16)
        v = jnp.concatenate([vn_ref[...], jnp.zeros((npad, W), F32)], axis=0).astype(BF16)
        s = lax.dot_general(qbd_ref[...], k, _NT, preferred_element_type=F32)
        if mode == "fox":
            s = s + cq_ref[...] - expand_rows(ckn_ref[...])
        tpos = lax.broadcasted_iota(jnp.int32, (R, LANES), 0) % tnew
        spos = lax.broadcasted_iota(jnp.int32, (R, LANES), 1)
        if mode == "fox":
            visible = spos <= tpos
        else:
            visible = ((past + spos) // CHUNK) <= ((past + tpos) // CHUNK)
        visible = jnp.logical_and(visible, spos < tnew)
        update(jnp.where(visible, s, NEG_INF),
               lambda p: jnp.dot(p, v, preferred_element_type=F32))

        accn = acc_ref[...] / l_ref[...]
        if mode == "fox":
            lane_map = lax.broadcasted_iota(jnp.int32, (tnew, W), 1) // HEAD_DIM
            o = jnp.zeros((tnew, W), F32)
            for r in range(nmap):
                o = jnp.where(lane_map == r, accn[r * tnew:(r + 1) * tnew, :], o)
            o_ref[...] = o * jax.nn.sigmoid(gate_ref[...])
        else:
            lam = _diff_lambda(lam_ref, lam_init)
            for h in range(nmap // 2):
                sl = slice(h * LANES, (h + 1) * LANES)
                o = (accn[(2 * h) * tnew:(2 * h + 1) * tnew, sl]
                     - lam * accn[(2 * h + 1) * tnew:(2 * h + 2) * tnew, sl])
                y = o * lax.rsqrt(jnp.mean(o * o, axis=-1, keepdims=True) + EPS)
                o_ref[:, sl] = (y * subg_ref[...]) * (1.0 - lam_init)


def sample_attention(mode, out_buf, q, k, vsrc, v_colblk, kcache_t, vcache, v_time_minor,
                     layer_idx, nstream, tnew, row0, tk, *, cq=None, ck_cache=None, ck_new=None,
                     gate_src=None, gate_colblk=None, lam=None, sub_g=None, lam_init=0.0):
    W = q.shape[1]
    past = kcache_t.shape[2]
    nk = past // tk
    nmap = W // HEAD_DIM
    R = nmap * tnew
    rb0 = row0 // tnew
    wblk = lambda colblk: pl.BlockSpec((tnew, W), lambda b, s, c=colblk: (rb0 + b, c))
    time_minor = pl.BlockSpec((None, W, tk),
                              lambda b, s: (layer_idx * nstream + b, 0, jnp.minimum(s, nk - 1)))
    time_major = pl.BlockSpec((None, tk) + vcache.shape[2:],
                              lambda b, s: (layer_idx * nstream + b, jnp.minimum(s, nk - 1), 0, 0))
    in_specs = [wblk(0), time_minor, time_minor if v_time_minor else time_major,
                wblk(0), wblk(v_colblk)]
    args = [q, kcache_t, vcache, k, vsrc]
    if mode == "fox":
        in_specs += [pl.BlockSpec((None, R, 1), lambda b, s: (b, 0, 0)),
                     pl.BlockSpec((None, nmap, tk), lambda b, s: (b, 0, jnp.minimum(s, nk - 1))),
                     pl.BlockSpec((None, nmap, LANES), lambda b, s: (b, 0, 0)),
                     wblk(gate_colblk)]
        args += [cq, ck_cache, ck_new, gate_src]
    else:
        in_specs += [pl.BlockSpec(lam.shape, lambda b, s: (0, 0)),
                     pl.BlockSpec((1, LANES), lambda b, s: (0, 0))]
        args += [lam, sub_g]
    return pl.pallas_call(
        functools.partial(_sample_attn_kernel, mode=mode, lam_init=lam_init,
                          scale=1.0 / math.sqrt(HEAD_DIM), past=past, v_time_minor=v_time_minor),
        grid=(nstream, nk + 1),
        in_specs=[pl.BlockSpec(memory_space=pl.ANY)] + in_specs,
        out_specs=pl.BlockSpec((tnew, W), lambda b, s: (rb0 + b, 0)),
        out_shape=jax.ShapeDtypeStruct(out_buf.shape, F32),
        input_output_aliases={0: 0},
        scratch_shapes=[pltpu.VMEM((R, W), BF16), pltpu.VMEM((R, 1), F32),
                        pltpu.VMEM((R, 1), F32), pltpu.VMEM((R, W), F32)],
        compiler_params=_params(("arbitrary", "arbitrary")),
        name=f"sample_attention_{mode}",
    )(out_buf, *args)


def _top16_desc(e):
    rows = []
    for _ in range(PEER_TOPK):
        m = jnp.max(e, axis=0, keepdims=True)
        rows.append(m)
        e = jnp.where(e == m, -1.0, e)
    return jnp.concatenate(rows, axis=0)


def _compare_exchange(v, i, l):
    v[i], v[l] = jnp.maximum(v[i], v[l]), jnp.minimum(v[i], v[l])


def _top16_ranks(e):
    n = e.shape[0] // SUBLANES
    assert n == PEER_TOPK
    v = [e[r * SUBLANES:(r + 1) * SUBLANES, :] for r in range(n)]
    k = 2
    while k <= n:
        j = k // 2
        while j >= 1:
            for i in range(n):
                l = i ^ j
                if l > i:
                    if i & k == 0:
                        _compare_exchange(v, i, l)
                    else:
                        _compare_exchange(v, l, i)
            j //= 2
        k *= 2
    shift = SUBLANES // 2
    while shift >= 1:
        w = [pltpu.roll(x, shift, axis=0) for x in v]
        v = [jnp.maximum(v[i], w[n - 1 - i]) for i in range(n)]
        j = n // 2
        while j >= 1:
            for i in range(n):
                l = i ^ j
                if l > i:
                    _compare_exchange(v, i, l)
            j //= 2
        shift //= 2
    return v


def _ranks_on_sublanes(v, first):
    row = lax.broadcasted_iota(jnp.int32, v[0].shape, 0)
    out = v[first + SUBLANES - 1]
    for r in range(SUBLANES - 2, -1, -1):
        out = jnp.where(row == r, v[first + r], out)
    return out


def _pair_candidates(a, b):
    row = lax.broadcasted_iota(jnp.int32, a[0].shape, 0)
    b_lo = _ranks_on_sublanes(b, 0)
    b_hi = _ranks_on_sublanes(b, SUBLANES)
    parts = [b_lo * a[0], b_hi * a[0], b_lo * a[1]]
    for r1 in range(2, SUBLANES):
        parts.append(jnp.where(row < PEER_TOPK // (r1 + 1), b_lo * a[r1], -1.0))
    parts.append(b[0] * _ranks_on_sublanes(a, SUBLANES))
    return jnp.concatenate(parts, axis=0)


def _peer_kernel(x_ref, g_ref, wq_ref, sk_ref, u0_ref, un_ref, vp_ref, vl_ref, o_ref,
                 hn_ref, e1_ref, e2_ref, thr_ref, at0_ref, at1_ref, wt0_ref, wt1_ref, yt_ref,
                 *, nheads):
    tb, D = x_ref.shape
    te = un_ref.shape[0]
    nkeys = sk_ref.shape[1]
    half = sk_ref.shape[2]
    e_step = pl.program_id(1)
    ne = pl.num_programs(1)
    assert te == SUBLANES * nkeys
    chunk_w = min(MXU_WIDTH, tb)
    nchunk = tb // chunk_w

    @pl.when(e_step == 0)
    def _():
        x = x_ref[...]
        y = x * lax.rsqrt(jnp.mean(x * x, axis=-1, keepdims=True) + EPS)
        hn_ref[...] = (y * g_ref[...]).astype(BF16)

        def head_body(h, carry):
            es = []
            for c in range(2):
                r0 = pl.multiple_of((h * 2 + c) * half, half)
                qs = lax.dot_general(wq_ref[pl.ds(r0, half), :], hn_ref[...], _NT,
                                     preferred_element_type=F32).astype(BF16)
                s = jnp.dot(sk_ref[h * 2 + c], qs, preferred_element_type=F32)
                es.append(jnp.exp(s - jnp.max(s, axis=0, keepdims=True)))
            e2_ref[h] = es[1]
            tiny = jnp.finfo(F32).tiny
            for tc in range(tb // LANES):
                sl = slice(tc * LANES, (tc + 1) * LANES)
                e1 = es[0][:, sl]
                a = _top16_ranks(e1)
                b = _top16_ranks(es[1][:, sl])
                cand = _pair_candidates(a, b)
                top = jnp.maximum(_top16_desc(cand), 0.0)
                rz = 1.0 / jnp.sum(top, axis=0, keepdims=True)
                winner = cand >= jnp.maximum(top[PEER_TOPK - 1:PEER_TOPK, :], tiny)
                cand_n = _pair_candidates([ar * rz for ar in a], b)
                thr = jnp.min(jnp.where(winner, cand_n, 2.0), axis=0, keepdims=True)
                e1_ref[h, :, sl] = e1 * rz
                thr_ref[h, :, sl] = jnp.broadcast_to(jnp.maximum(thr, tiny), (SUBLANES, LANES))
            return carry

        lax.fori_loop(0, nheads, head_body, 0)
        at0_ref[...] = lax.dot_general(u0_ref[...], hn_ref[...], _NT, preferred_element_type=F32)
        wt1_ref[...] = jnp.zeros(wt1_ref.shape, BF16)
        yt_ref[...] = jnp.zeros(yt_ref.shape, F32)

    pieces = ([("pre", r, c) for c in range(nchunk) for r in range(te // PIECE_ROWS)]
              + [("out", r, c) for c in range(nchunk) for r in range(D // PIECE_ROWS)])

    def mxu_piece(piece, at_nxt, wt_prv):
        kind, r, c = piece
        rows = slice(r * PIECE_ROWS, (r + 1) * PIECE_ROWS)
        cols = slice(c * chunk_w, (c + 1) * chunk_w)
        if kind == "pre":
            at_nxt[rows, cols] = lax.dot_general(un_ref[rows, :], hn_ref[cols, :], _NT,
                                                 preferred_element_type=F32)
        else:
            yt_ref[rows, cols] += jnp.dot(vp_ref[rows, :], wt_prv[:, cols],
                                          preferred_element_type=F32)

    def gate_unit(tc, ii, e1_tiles, at_cur, wt_cur):
        sl = slice(tc * LANES, (tc + 1) * LANES)
        for jh in range(nkeys // GATE_ROWS):
            js = slice(jh * GATE_ROWS, (jh + 1) * GATE_ROWS)
            gate = jnp.zeros((GATE_ROWS, LANES), F32)
            for h in range(nheads):
                t = e2_ref[h, js, sl] * e1_tiles[h][ii:ii + 1, :]
                gate = gate + jnp.where(t >= thr_ref[h, 0:1, sl], t, 0.0)
            rows = slice(ii * nkeys + jh * GATE_ROWS, ii * nkeys + (jh + 1) * GATE_ROWS)
            a = at_cur[rows, sl]
            act = 0.5 * a * (1.0 + lax.erf(a * (1.0 / math.sqrt(2.0))))
            wt_cur[rows, sl] = (gate * act).astype(BF16)

    def body(at_cur, at_nxt, wt_cur, wt_prv):
        i0 = pl.multiple_of(e_step * SUBLANES, SUBLANES)
        n_units = (tb // LANES) * SUBLANES
        done = 0
        for tc in range(tb // LANES):
            sl = slice(tc * LANES, (tc + 1) * LANES)
            e1_tiles = [e1_ref[h, pl.ds(i0, SUBLANES), sl] for h in range(nheads)]
            for ii in range(SUBLANES):
                unit = tc * SUBLANES + ii
                upto = min(len(pieces), unit * len(pieces) // n_units + 1)
                for piece in pieces[done:upto]:
                    mxu_piece(piece, at_nxt, wt_prv)
                done = upto
                gate_unit(tc, ii, e1_tiles, at_cur, wt_cur)

    @pl.when(e_step % 2 == 0)
    def _():
        body(at0_ref, at1_ref, wt0_ref, wt1_ref)

    @pl.when(e_step % 2 == 1)
    def _():
        body(at1_ref, at0_ref, wt1_ref, wt0_ref)

    @pl.when(e_step == ne - 1)
    def _():
        yt = yt_ref[...] + jnp.dot(vl_ref[...], wt1_ref[...], preferred_element_type=F32)
        o_ref[...] = x_ref[...] + yt.T


def peer_layer(x, g, layer, wq_t, sub_keys, u, v_t, tb):
    M, D = x.shape
    E = u.shape[1]
    nheads = sub_keys.shape[1] // 2
    nkeys = sub_keys.shape[2]
    te = SUBLANES * nkeys
    ne = E // te
    assert ne % 2 == 0
    return pl.pallas_call(
        functools.partial(_peer_kernel, nheads=nheads),
        grid=(M // tb, ne),
        in_specs=[pl.BlockSpec((tb, D), lambda i, e: (i, 0)),
                  pl.BlockSpec((1, D), lambda i, e: (0, 0)),
                  pl.BlockSpec((None,) + wq_t.shape[1:], lambda i, e: (layer, 0, 0)),
                  pl.BlockSpec((None,) + sub_keys.shape[1:], lambda i, e: (layer, 0, 0, 0)),
                  pl.BlockSpec((None, te, D), lambda i, e: (layer, 0, 0)),
                  pl.BlockSpec((None, te, D),
                               lambda i, e: (layer, jnp.minimum(e + 1, ne - 1), 0)),
                  pl.BlockSpec((None, D, te), lambda i, e: (layer, 0, jnp.maximum(e - 1, 0))),
                  pl.BlockSpec((None, D, te), lambda i, e: (layer, 0, ne - 1))],
        out_specs=pl.BlockSpec((tb, D), lambda i, e: (i, 0)),
        out_shape=jax.ShapeDtypeStruct((M, D), F32),
        scratch_shapes=[pltpu.VMEM((tb, D), BF16),
                        pltpu.VMEM((nheads, nkeys, tb), F32),
                        pltpu.VMEM((nheads, nkeys, tb), F32),
                        pltpu.VMEM((nheads, SUBLANES, tb), F32),
                        pltpu.VMEM((te, tb), F32),
                        pltpu.VMEM((te, tb), F32),
                        pltpu.VMEM((te, tb), BF16),
                        pltpu.VMEM((te, tb), BF16),
                        pltpu.VMEM((D, tb), F32)],
        compiler_params=_params(("arbitrary", "arbitrary")),
        name="peer",
    )(x, g, wq_t, sub_keys, u, u, v_t, v_t)


def _rope_tables(pos):
    inv = ROPE_THETA ** (-jnp.arange(0, HEAD_DIM, 2, dtype=F32) / HEAD_DIM)
    ang = pos.astype(F32)[:, None] * inv[None, :]
    ang = jnp.concatenate([ang, ang, ang, ang], axis=-1)
    return jnp.cos(ang), jnp.sin(ang)


def _tile_heads(g, width):
    return jnp.tile(g, width // g.shape[0]).reshape(1, width)


def _time_minor(cache):
    nd = cache.ndim
    c = jnp.transpose(cache, (0, 1) + tuple(range(3, nd)) + (2,))
    return c.reshape(c.shape[0] * c.shape[1], -1, c.shape[-1])


def _forward(x_prompt, x_sample, cache_fox_k, cache_fox_v, cache_fox_logf, cache_diff_k,
             cache_diff_v, norm_mix_g, norm_ffn_g, fox_w_in, fox_b_f, fox_q_g, fox_k_g, fox_w_o,
             diff_w_in, diff_q_g, diff_k_g, diff_lambda, diff_sub_g, diff_w_o,
             peer_w_q, peer_sub_keys, peer_u, peer_v, *, tm, tq, tk_cache, tb):
    B, S, D = x_prompt.shape
    NS, T, _ = x_sample.shape
    depth = norm_mix_g.shape[0]
    past = cache_fox_k.shape[2]
    MP = B * S
    M = MP + NS * T
    fox_heads = cache_fox_k.shape[3]
    diff_heads = cache_diff_k.shape[3]
    W = fox_heads * HEAD_DIM
    npair = W // LANES

    x = jnp.concatenate([x_prompt.reshape(MP, D), x_sample.reshape(NS * T, D)], axis=0)

    pos = jnp.concatenate([jnp.tile(jnp.arange(S, dtype=jnp.int32), B),
                           jnp.tile(past + jnp.arange(T, dtype=jnp.int32), NS)])
    cos, sin = _rope_tables(pos)

    kc_fox = _time_minor(cache_fox_k)
    vc_fox = _time_minor(cache_fox_v)
    kc_diff = _time_minor(cache_diff_k)
    vc_diff = cache_diff_v.reshape((-1, past) + cache_diff_v.shape[3:])
    lf_cache = jnp.swapaxes(cache_fox_logf, 2, 3)

    wq_t = jnp.swapaxes(peer_w_q, 1, 2).astype(BF16)
    sk_all = peer_sub_keys.reshape((depth, -1) + peer_sub_keys.shape[3:]).astype(BF16)
    u_all = peer_u.astype(BF16)
    v_t_all = jnp.swapaxes(peer_v, 1, 2).astype(BF16)

    n_fox, n_diff = cache_fox_k.shape[0], cache_diff_k.shape[0]
    outs = {k: [] for k in ("fl_p", "fk_s", "fv_s", "fl_s", "dk_s", "dv_s")}
    fk_buf = fv_buf = dk_buf = dv_buf = None

    for layer in range(depth):
        j = layer // 2
        g_mix = norm_mix_g[layer].reshape(1, D)
        if layer % 2 == 0:
            w_in = jnp.pad(fox_w_in[j], ((0, 0), (0, LANES - fox_heads))).astype(BF16)
            p = norm_matmul(x, g_mix, w_in, tm, w_in.shape[1])
            b_f = jnp.pad(fox_b_f[j], (0, LANES - fox_heads)).reshape(1, LANES)
            qn, kn, lf, lf_t = fox_post(p, b_f, _tile_heads(fox_q_g[j], W),
                                        _tile_heads(fox_k_g[j], W), tm)
            lf = lf[:, :fox_heads]
            lf_p = lf[:MP].reshape(B, S, fox_heads)
            lf_s = lf[MP:].reshape(NS, T, fox_heads)
            c_t = cumsum_lanes(lf_t[:, :MP], LANES, S)[:fox_heads]
            c_pt = jnp.swapaxes(c_t.reshape(fox_heads, B, S), 0, 1)
            c_rows = (c_pt * LOG2E).reshape(B, npair, 2, S)
            c_cols = jnp.swapaxes(c_rows, 2, 3)
            o_p = prompt_attention("fox", qn, kn, p, 2 * npair, B, S, tq,
                                   c_rows=c_rows, c_cols=c_cols, gate_src=p,
                                   gate_col0=3 * npair)
            c_cache = cumsum_lanes(lf_cache[j].reshape(NS * fox_heads, past), LANES, past)
            lf_new = jnp.pad(jnp.swapaxes(lf_s, 1, 2), ((0, 0), (0, 0), (0, LANES - T)))
            c_new = cumsum_lanes(lf_new.reshape(NS * fox_heads, LANES), LANES, LANES)
            ck_new = (c_cache[:, past - 1:] + c_new).reshape(NS, fox_heads, LANES)
            cq_s = ck_new[:, :, :T].reshape(NS, fox_heads * T, 1)
            o = sample_attention("fox", o_p, qn, kn, p, 2, kc_fox, vc_fox, True, j, NS, T, MP,
                                 tk_cache, cq=cq_s,
                                 ck_cache=c_cache.reshape(NS, fox_heads, past), ck_new=ck_new,
                                 gate_src=p, gate_colblk=3)
            x = matmul_residual(o, fox_w_o[j].astype(BF16), x, tm)
            fk_buf = emit_prompt_rows(kn, 0, W, j, n_fox, B, S, tm, True, fk_buf)
            fv_buf = emit_prompt_rows(p, 2, W, j, n_fox, B, S, tm, True, fv_buf)
            outs["fl_p"].append(lf_p)
            outs["fk_s"].append(kn[MP:].reshape(NS, T, fox_heads, HEAD_DIM))
            outs["fv_s"].append(p[MP:, 2 * W:3 * W].reshape(NS, T, fox_heads, HEAD_DIM))
            outs["fl_s"].append(lf_s)
        else:
            lam_init = 0.8 - 0.6 * math.exp(-0.3 * layer)
            p = norm_matmul(x, g_mix, diff_w_in[j].astype(BF16), tm, 3 * W)
            qn, kn = diff_post(p, cos, sin, _tile_heads(diff_q_g[j], W),
                               _tile_heads(diff_k_g[j], W), tm)
            sub_g = diff_sub_g[j].reshape(1, LANES)
            o_p = prompt_attention("diff", qn, kn, p, 2 * npair, B, S, tq,
                                   lam=diff_lambda[j], sub_g=sub_g, lam_init=lam_init)
            o = sample_attention("diff", o_p, qn, kn, p, 2, kc_diff, vc_diff, False, j, NS, T,
                                 MP, tk_cache, lam=diff_lambda[j], sub_g=sub_g,
                                 lam_init=lam_init)
            x = matmul_residual(o, diff_w_o[j].astype(BF16), x, tm)
            dk_buf = emit_prompt_rows(kn, 0, W, j, n_diff, B, S, tm, True, dk_buf)
            dv_buf = emit_prompt_rows(p, 2, W, j, n_diff, B, S, tm, False, dv_buf)
            outs["dk_s"].append(kn[MP:].reshape(NS, T, diff_heads, 2, HEAD_DIM))
            outs["dv_s"].append(p[MP:, 2 * W:3 * W].reshape(NS, T, diff_heads, 2 * HEAD_DIM))

        x = peer_layer(x, norm_ffn_g[layer].reshape(1, D), layer, wq_t, sk_all, u_all, v_t_all,
                       tb)

    fk_p = jnp.transpose(fk_buf.reshape(n_fox, B, fox_heads, HEAD_DIM, S), (0, 1, 4, 2, 3))
    fv_p = jnp.transpose(fv_buf.reshape(n_fox, B, fox_heads, HEAD_DIM, S), (0, 1, 4, 2, 3))
    dk_p = jnp.transpose(dk_buf.reshape(n_diff, B, diff_heads, 2, HEAD_DIM, S),
                         (0, 1, 5, 2, 3, 4))
    dv_p = dv_buf.reshape(n_diff, B, S, diff_heads, 2 * HEAD_DIM)
    return (x[:MP].reshape(B, S, D), x[MP:].reshape(NS, T, D),
            fk_p, fv_p, jnp.stack(outs["fl_p"]), dk_p, dv_p,
            jnp.stack(outs["fk_s"]), jnp.stack(outs["fv_s"]), jnp.stack(outs["fl_s"]),
            jnp.stack(outs["dk_s"]), jnp.stack(outs["dv_s"]))


def kernel(x_prompt, x_sample, cache_fox_k, cache_fox_v, cache_fox_logf, cache_diff_k, cache_diff_v, norm_mix_g, norm_ffn_g, fox_w_in, fox_b_f, fox_q_g, fox_k_g, fox_w_o, diff_w_in, diff_q_g, diff_k_g, diff_lambda, diff_sub_g, diff_w_o, peer_w_q, peer_sub_keys, peer_u, peer_v):
    return _forward(x_prompt, x_sample, cache_fox_k, cache_fox_v, cache_fox_logf, cache_diff_k,
                    cache_diff_v, norm_mix_g, norm_ffn_g, fox_w_in, fox_b_f, fox_q_g, fox_k_g,
                    fox_w_o, diff_w_in, diff_q_g, diff_k_g, diff_lambda, diff_sub_g, diff_w_o,
                    peer_w_q, peer_sub_keys, peer_u, peer_v,
                    tm=512, tq=512, tk_cache=512, tb=768)
```

```python
import functools
import math

import jax
import jax.numpy as jnp
from jax import lax
from jax.experimental import pallas as pl
from jax.experimental.pallas import tpu as pltpu

F32 = jnp.float32
BF16 = jnp.bfloat16
EPS = 1e-6
NEG_INF = -1e30
CHUNK = 64
ROPE_THETA = 10000.0
LOG2E = math.log2(math.e)
PEER_TOPK = 16
LANES = 128
SUBLANES = 8
MXU_WIDTH = 256
GATE_ROWS = 32
PIECE_ROWS = 1024
PAIRS_PER_STEP = 4
HEAD_DIM = 64
VMEM_LIMIT = 56 * 1024 * 1024

_NT = (((1,), (1,)), ((), ()))


def _params(sem, vmem=VMEM_LIMIT):
    return pltpu.CompilerParams(dimension_semantics=sem, vmem_limit_bytes=vmem)


def _norm_matmul_kernel(x_ref, g_ref, w_ref, o_ref, hn_ref):
    @pl.when(pl.program_id(1) == 0)
    def _():
        x = x_ref[...]
        y = x * lax.rsqrt(jnp.mean(x * x, axis=-1, keepdims=True) + EPS)
        hn_ref[...] = (y * g_ref[...]).astype(BF16)

    o_ref[...] = jnp.dot(hn_ref[...], w_ref[...], preferred_element_type=F32)


def norm_matmul(x, g, w, tm, tn):
    M, D = x.shape
    N = w.shape[1]
    return pl.pallas_call(
        _norm_matmul_kernel,
        grid=(M // tm, N // tn),
        in_specs=[pl.BlockSpec((tm, D), lambda i, j: (i, 0)),
                  pl.BlockSpec((1, D), lambda i, j: (0, 0)),
                  pl.BlockSpec((D, tn), lambda i, j: (0, j))],
        out_specs=pl.BlockSpec((tm, tn), lambda i, j: (i, j)),
        out_shape=jax.ShapeDtypeStruct((M, N), F32),
        scratch_shapes=[pltpu.VMEM((tm, D), BF16)],
        compiler_params=_params(("arbitrary", "arbitrary")),
        name="norm_matmul",
    )(x, g, w)


def _matmul_res_kernel(a_ref, w_ref, r_ref, o_ref):
    o_ref[...] = r_ref[...] + jnp.dot(a_ref[...].astype(BF16), w_ref[...],
                                      preferred_element_type=F32)


def matmul_residual(a, w, res, tm):
    M, K = a.shape
    N = w.shape[1]
    return pl.pallas_call(
        _matmul_res_kernel,
        grid=(M // tm,),
        in_specs=[pl.BlockSpec((tm, K), lambda i: (i, 0)),
                  pl.BlockSpec((K, N), lambda i: (0, 0)),
                  pl.BlockSpec((tm, N), lambda i: (i, 0))],
        out_specs=pl.BlockSpec((tm, N), lambda i: (i, 0)),
        out_shape=jax.ShapeDtypeStruct((M, N), F32),
        compiler_params=_params(("arbitrary",)),
        name="matmul_residual",
    )(a, w, res)


def _emit_kernel(src_ref, *rest, transpose):
    o_ref = rest[-1]
    o_ref[...] = src_ref[...].T if transpose else src_ref[...]


def emit_prompt_rows(src, colblk, width, layer, nlayers, batch, seq, tm, transpose, buf=None):
    nblk = seq // tm
    in_specs = [pl.BlockSpec((tm, width), lambda b, s: (b * nblk + s, colblk))]
    args = [src]
    if buf is not None:
        in_specs.append(pl.BlockSpec(memory_space=pl.ANY))
        args.append(buf)
    if transpose:
        out_shape = (nlayers, batch, width, seq)
        out_spec = pl.BlockSpec((None, None, width, tm), lambda b, s: (layer, b, 0, s))
    else:
        out_shape = (nlayers, batch, seq, width)
        out_spec = pl.BlockSpec((None, None, tm, width), lambda b, s: (layer, b, s, 0))
    return pl.pallas_call(
        functools.partial(_emit_kernel, transpose=transpose),
        grid=(batch, nblk),
        in_specs=in_specs,
        out_specs=out_spec,
        out_shape=jax.ShapeDtypeStruct(out_shape, F32),
        input_output_aliases={} if buf is None else {1: 0},
        compiler_params=_params(("arbitrary", "arbitrary")),
        name="emit_prompt_rows",
    )(*args)


def _head_rms64(xc, lo):
    sq = xc * xc
    s_lo = jnp.sum(jnp.where(lo, sq, 0.0), axis=-1, keepdims=True)
    s_hi = jnp.sum(jnp.where(lo, 0.0, sq), axis=-1, keepdims=True)
    ms = jnp.where(lo, s_lo, s_hi) * (1.0 / HEAD_DIM)
    return lax.rsqrt(ms + EPS)


def _fox_post_kernel(q_ref, k_ref, f_ref, bf_ref, qg_ref, kg_ref, qo_ref, ko_ref, lf_ref,
                     lft_ref):
    tm, width = q_ref.shape
    lo = lax.broadcasted_iota(jnp.int32, (tm, LANES), 1) < HEAD_DIM
    for src, g_ref, dst in ((q_ref, qg_ref, qo_ref), (k_ref, kg_ref, ko_ref)):
        for c in range(width // LANES):
            sl = slice(c * LANES, (c + 1) * LANES)
            xc = src[:, sl]
            dst[:, sl] = xc * _head_rms64(xc, lo) * g_ref[:, sl]
    z = f_ref[...] + bf_ref[...]
    lf = jnp.minimum(z, 0.0) - jnp.log1p(jnp.exp(-jnp.abs(z)))
    lf_ref[...] = lf
    lft_ref[...] = lf.T


def fox_post(p, b_f, q_g, k_g, tm):
    M = p.shape[0]
    W = q_g.shape[1]
    row = lambda c: pl.BlockSpec((tm, W), lambda i, c=c: (i, c))
    vec = lambda n: pl.BlockSpec((1, n), lambda i: (0, 0))
    return pl.pallas_call(
        _fox_post_kernel,
        grid=(M // tm,),
        in_specs=[row(0), row(1), pl.BlockSpec((tm, LANES), lambda i: (i, 4 * W // LANES)),
                  vec(LANES), vec(W), vec(W)],
        out_specs=[row(0), row(0), pl.BlockSpec((tm, LANES), lambda i: (i, 0)),
                   pl.BlockSpec((LANES, tm), lambda i: (0, i))],
        out_shape=[jax.ShapeDtypeStruct((M, W), F32), jax.ShapeDtypeStruct((M, W), F32),
                   jax.ShapeDtypeStruct((M, LANES), F32), jax.ShapeDtypeStruct((LANES, M), F32)],
        compiler_params=_params(("arbitrary",)),
        name="fox_post",
    )(p, p, p, b_f, q_g, k_g)


def _diff_post_kernel(q_ref, k_ref, cos_ref, sin_ref, qg_ref, kg_ref, qo_ref, ko_ref):
    tm, width = q_ref.shape
    lane = lax.broadcasted_iota(jnp.int32, (tm, LANES), 1)
    lo = lane < HEAD_DIM
    first_half = (lane % HEAD_DIM) < (HEAD_DIM // 2)
    cos = cos_ref[...]
    sin = sin_ref[...]
    for src, g_ref, dst in ((q_ref, qg_ref, qo_ref), (k_ref, kg_ref, ko_ref)):
        for c in range(width // LANES):
            sl = slice(c * LANES, (c + 1) * LANES)
            xc = src[:, sl]
            xn = xc * _head_rms64(xc, lo) * g_ref[:, sl]
            rot = jnp.where(first_half,
                            -pltpu.roll(xn, LANES - HEAD_DIM // 2, axis=1),
                            pltpu.roll(xn, HEAD_DIM // 2, axis=1))
            dst[:, sl] = xn * cos + rot * sin


def diff_post(p, cos, sin, q_g, k_g, tm):
    M = p.shape[0]
    W = q_g.shape[1]
    row = lambda c: pl.BlockSpec((tm, W), lambda i, c=c: (i, c))
    tab = pl.BlockSpec((tm, LANES), lambda i: (i, 0))
    vec = pl.BlockSpec((1, W), lambda i: (0, 0))
    return pl.pallas_call(
        _diff_post_kernel,
        grid=(M // tm,),
        in_specs=[row(0), row(1), tab, tab, vec, vec],
        out_specs=[row(0), row(0)],
        out_shape=[jax.ShapeDtypeStruct((M, W), F32), jax.ShapeDtypeStruct((M, W), F32)],
        compiler_params=_params(("arbitrary",)),
        name="diff_post",
    )(p, p, cos, sin, q_g, k_g)


def _cumsum_kernel(x_ref, o_ref):
    rows, length = x_ref.shape
    s_idx = lax.broadcasted_iota(jnp.int32, (LANES, LANES), 0)
    t_idx = lax.broadcasted_iota(jnp.int32, (LANES, LANES), 1)
    tri = (s_idx <= t_idx).astype(F32)
    carry = jnp.zeros((rows, 1), F32)
    for blk in range(length // LANES):
        sl = slice(blk * LANES, (blk + 1) * LANES)
        c = jnp.dot(x_ref[:, sl], tri, preferred_element_type=F32,
                    precision=lax.Precision.HIGHEST) + carry
        o_ref[:, sl] = c
        carry = c[:, LANES - 1:LANES]


def cumsum_lanes(x, rows, length):
    R, L = x.shape
    return pl.pallas_call(
        _cumsum_kernel,
        grid=(R // rows, L // length),
        in_specs=[pl.BlockSpec((rows, length), lambda i, j: (i, j))],
        out_specs=pl.BlockSpec((rows, length), lambda i, j: (i, j)),
        out_shape=jax.ShapeDtypeStruct((R, L), F32),
        compiler_params=_params(("arbitrary", "arbitrary")),
        name="cumsum_lanes",
    )(x)


def _diff_lambda(lam_ref, lam_init):
    lf = lam_ref[...]
    a = jnp.sum(lf[0:1, :] * lf[1:2, :], axis=-1, keepdims=True)
    b = jnp.sum(lf[2:3, :] * lf[3:4, :], axis=-1, keepdims=True)
    return jnp.exp(a) - jnp.exp(b) + lam_init


def _prompt_attn_kernel(qi_tab, ki_tab, q_ref, k_ref, v_ref, *rest, mode, lam_init, scale):
    if mode == "fox":
        cq_ref, ck_ref, gate_ref, o_ref = rest[:4]
        scratch = rest[4:]
    else:
        lam_ref, subg_ref, o_ref = rest[:3]
        scratch = rest[3:]
    qs_ref, m_ref, l_ref, acc_ref = scratch
    tq = q_ref.shape[0]
    tk = k_ref.shape[0]
    pairs = q_ref.shape[1] // LANES
    t = pl.program_id(2)
    qi = qi_tab[t]
    ki = ki_tab[t]

    @pl.when(ki == 0)
    def _():
        qs_ref[...] = (q_ref[...] * scale).astype(BF16)
        m_ref[...] = jnp.full(m_ref.shape, NEG_INF, F32)
        l_ref[...] = jnp.zeros(l_ref.shape, F32)
        acc_ref[...] = jnp.zeros(acc_ref.shape, F32)

    def update(s, vt, pp, mp):
        m_prev = m_ref[pp, mp]
        m_new = jnp.maximum(m_prev, jnp.max(s, axis=0, keepdims=True))
        alpha = jnp.exp2(m_prev - m_new)
        p = jnp.exp2(s - m_new)
        l_ref[pp, mp] = alpha * l_ref[pp, mp] + jnp.sum(p, axis=0, keepdims=True)
        acc_ref[pp, mp] = alpha * acc_ref[pp, mp] + jnp.dot(vt, p.astype(BF16),
                                                            preferred_element_type=F32)
        m_ref[pp, mp] = m_new

    def step(pp, diagonal):
        sl = slice(pp * LANES, (pp + 1) * LANES)
        k = k_ref[:, sl]
        lo_k = lax.broadcasted_iota(jnp.int32, (tk, LANES), 1) < HEAD_DIM
        k0 = jnp.where(lo_k, k, 0.0).astype(BF16)
        k1 = jnp.where(lo_k, 0.0, k).astype(BF16)
        qs = qs_ref[:, sl]
        s0 = lax.dot_general(k0, qs, _NT, preferred_element_type=F32)
        s1 = lax.dot_general(k1, qs, _NT, preferred_element_type=F32)
        if mode == "fox":
            s0 = s0 + cq_ref[pp, 0:1, :] - ck_ref[pp, :, 0:1]
            s1 = s1 + cq_ref[pp, 1:2, :] - ck_ref[pp, :, 1:2]
        if diagonal:
            key = lax.broadcasted_iota(jnp.int32, (tk, tq), 0)
            qry = lax.broadcasted_iota(jnp.int32, (tk, tq), 1)
            if mode == "fox":
                visible = key <= qry
            else:
                visible = (key // CHUNK) <= (qry // CHUNK)
            s0 = jnp.where(visible, s0, NEG_INF)
            s1 = jnp.where(visible, s1, NEG_INF)
        vt = v_ref[:, sl].T.astype(BF16)
        update(s0, vt, pp, 0)
        update(s1, vt, pp, 1)

    def finish(pp):
        sl = slice(pp * LANES, (pp + 1) * LANES)
        o0 = acc_ref[pp, 0] / l_ref[pp, 0]
        o1 = acc_ref[pp, 1] / l_ref[pp, 1]
        if mode == "fox":
            first_head = lax.broadcasted_iota(jnp.int32, (LANES, tq), 0) < HEAD_DIM
            ot = jnp.where(first_head, o0, o1)
            o_ref[:, sl] = ot.T * jax.nn.sigmoid(gate_ref[:, sl])
        else:
            ot = o0 - _diff_lambda(lam_ref, lam_init) * o1
            y = ot * lax.rsqrt(jnp.mean(ot * ot, axis=0, keepdims=True) + EPS)
            o_ref[:, sl] = (y.T * subg_ref[...]) * (1.0 - lam_init)

    @pl.when(ki < qi)
    def _():
        for pp in range(pairs):
            step(pp, False)

    @pl.when(ki == qi)
    def _():
        for pp in range(pairs):
            step(pp, True)
            finish(pp)


def prompt_attention(mode, q, k, vsrc, v_col0, batch, seq, tq, *, c_rows=None, c_cols=None,
                     gate_src=None, gate_col0=None, lam=None, sub_g=None, lam_init=0.0):
    W = q.shape[1]
    npair = W // LANES
    nq = seq // tq
    tri = [(a, b) for a in range(nq) for b in range(a + 1)]
    qi_tab = jnp.asarray([a for a, _ in tri], jnp.int32)
    ki_tab = jnp.asarray([b for _, b in tri], jnp.int32)
    pp = math.gcd(PAIRS_PER_STEP, npair)
    assert v_col0 % pp == 0 and (gate_col0 or 0) % pp == 0
    wide = pp * LANES
    qmap = lambda b, h, t, qt, kt: (b * nq + qt[t], h)
    kmap = lambda b, h, t, qt, kt: (b * nq + kt[t], h)
    in_specs = [pl.BlockSpec((tq, wide), qmap),
                pl.BlockSpec((tq, wide), kmap),
                pl.BlockSpec((tq, wide),
                             lambda b, h, t, qt, kt: (b * nq + kt[t], v_col0 // pp + h))]
    args = [q, k, vsrc]
    if mode == "fox":
        in_specs += [pl.BlockSpec((None, pp, 2, tq), lambda b, h, t, qt, kt: (b, h, 0, qt[t])),
                     pl.BlockSpec((None, pp, tq, 2), lambda b, h, t, qt, kt: (b, h, kt[t], 0)),
                     pl.BlockSpec((tq, wide),
                                  lambda b, h, t, qt, kt: (b * nq + qt[t], gate_col0 // pp + h))]
        args += [c_rows, c_cols, gate_src]
    else:
        in_specs += [pl.BlockSpec(lam.shape, lambda b, h, t, qt, kt: (0, 0)),
                     pl.BlockSpec((1, LANES), lambda b, h, t, qt, kt: (0, 0))]
        args += [lam, sub_g]
    grid_spec = pltpu.PrefetchScalarGridSpec(
        num_scalar_prefetch=2,
        grid=(batch, npair // pp, len(tri)),
        in_specs=in_specs,
        out_specs=pl.BlockSpec((tq, wide), qmap),
        scratch_shapes=[pltpu.VMEM((tq, wide), BF16),
                        pltpu.VMEM((pp, 2, 1, tq), F32), pltpu.VMEM((pp, 2, 1, tq), F32),
                        pltpu.VMEM((pp, 2, LANES, tq), F32)],
    )
    return pl.pallas_call(
        functools.partial(_prompt_attn_kernel, mode=mode, lam_init=lam_init,
                          scale=LOG2E / math.sqrt(HEAD_DIM)),
        grid_spec=grid_spec,
        out_shape=jax.ShapeDtypeStruct(q.shape, F32),
        compiler_params=_params(("arbitrary", "arbitrary", "arbitrary")),
        name=f"prompt_attention_{mode}",
    )(qi_tab, ki_tab, *args)


def _sample_attn_kernel(buf_ref, q_ref, kc_ref, vc_ref, kn_ref, vn_ref, *rest, mode, lam_init,
                        scale, past, v_time_minor):
    del buf_ref
    if mode == "fox":
        cq_ref, ckc_ref, ckn_ref, gate_ref, o_ref = rest[:5]
        scratch = rest[5:]
    else:
        lam_ref, subg_ref, o_ref = rest[:3]
        scratch = rest[3:]
    qbd_ref, m_ref, l_ref, acc_ref = scratch
    tnew, W = q_ref.shape
    nmap = W // HEAD_DIM
    R = nmap * tnew
    kstep = pl.program_id(1)
    nk = pl.num_programs(1) - 1

    @pl.when(kstep == 0)
    def _():
        q = q_ref[...] * scale
        qt = jnp.concatenate([q] * nmap, axis=0)
        rmap = lax.broadcasted_iota(jnp.int32, (R, W), 0) // tnew
        cmap = lax.broadcasted_iota(jnp.int32, (R, W), 1) // HEAD_DIM
        qbd_ref[...] = jnp.where(rmap == cmap, qt, 0.0).astype(BF16)
        m_ref[...] = jnp.full(m_ref.shape, NEG_INF, F32)
        l_ref[...] = jnp.zeros(l_ref.shape, F32)
        acc_ref[...] = jnp.zeros(acc_ref.shape, F32)

    def expand_rows(c):
        n = c.shape[1]
        return jnp.broadcast_to(c[:, None, :], (nmap, tnew, n)).reshape(R, n)

    def update(s, pv):
        m_prev = m_ref[...]
        m_new = jnp.maximum(m_prev, jnp.max(s, axis=-1, keepdims=True))
        alpha = jnp.exp(m_prev - m_new)
        p = jnp.exp(s - m_new)
        l_ref[...] = alpha * l_ref[...] + jnp.sum(p, axis=-1, keepdims=True)
        acc_ref[...] = alpha * acc_ref[...] + pv(p.astype(BF16))
        m_ref[...] = m_new

    @pl.when(kstep < nk)
    def _():
        s = jnp.dot(qbd_ref[...], kc_ref[...].astype(BF16), preferred_element_type=F32)
        if mode == "fox":
            s = s + cq_ref[...] - expand_rows(ckc_ref[...])
        if v_time_minor:
            v = vc_ref[...].astype(BF16)
            update(s, lambda p: lax.dot_general(p, v, _NT, preferred_element_type=F32))
        else:
            v = jnp.concatenate([vc_ref[:, h, :] for h in range(vc_ref.shape[1])],
                                axis=1).astype(BF16)
            update(s, lambda p: jnp.dot(p, v, preferred_element_type=F32))

    @pl.when(kstep == nk)
    def _():
        npad = LANES - tnew
        k = jnp.concatenate([kn_ref[...], jnp.zeros((npad, W), F32)], axis=0).astype(BF16)
        v = jnp.concatenate([vn_ref[...], jnp.zeros((npad, W), F32)], axis=0).astype(BF16)
        s = lax.dot_general(qbd_ref[...], k, _NT, preferred_element_type=F32)
        if mode == "fox":
            s = s + cq_ref[...] - expand_rows(ckn_ref[...])
        tpos = lax.broadcasted_iota(jnp.int32, (R, LANES), 0) % tnew
        spos = lax.broadcasted_iota(jnp.int32, (R, LANES), 1)
        if mode == "fox":
            visible = spos <= tpos
        else:
            visible = ((past + spos) // CHUNK) <= ((past + tpos) // CHUNK)
        visible = jnp.logical_and(visible, spos < tnew)
        update(jnp.where(visible, s, NEG_INF),
               lambda p: jnp.dot(p, v, preferred_element_type=F32))

        accn = acc_ref[...] / l_ref[...]
        if mode == "fox":
            lane_map = lax.broadcasted_iota(jnp.int32, (tnew, W), 1) // HEAD_DIM
            o = jnp.zeros((tnew, W), F32)
            for r in range(nmap):
                o = jnp.where(lane_map == r, accn[r * tnew:(r + 1) * tnew, :], o)
            o_ref[...] = o * jax.nn.sigmoid(gate_ref[...])
        else:
            lam = _diff_lambda(lam_ref, lam_init)
            for h in range(nmap // 2):
                sl = slice(h * LANES, (h + 1) * LANES)
                o = (accn[(2 * h) * tnew:(2 * h + 1) * tnew, sl]
                     - lam * accn[(2 * h + 1) * tnew:(2 * h + 2) * tnew, sl])
                y = o * lax.rsqrt(jnp.mean(o * o, axis=-1, keepdims=True) + EPS)
                o_ref[:, sl] = (y * subg_ref[...]) * (1.0 - lam_init)


def sample_attention(mode, out_buf, q, k, vsrc, v_colblk, kcache_t, vcache, v_time_minor,
                     layer_idx, nstream, tnew, row0, tk, *, cq=None, ck_cache=None, ck_new=None,
                     gate_src=None, gate_colblk=None, lam=None, sub_g=None, lam_init=0.0):
    W = q.shape[1]
    past = kcache_t.shape[2]
    nk = past // tk
    nmap = W // HEAD_DIM
    R = nmap * tnew
    rb0 = row0 // tnew
    wblk = lambda colblk: pl.BlockSpec((tnew, W), lambda b, s, c=colblk: (rb0 + b, c))
    time_minor = pl.BlockSpec((None, W, tk),
                              lambda b, s: (layer_idx * nstream + b, 0, jnp.minimum(s, nk - 1)))
    time_major = pl.BlockSpec((None, tk) + vcache.shape[2:],
                              lambda b, s: (layer_idx * nstream + b, jnp.minimum(s, nk - 1), 0, 0))
    in_specs = [wblk(0), time_minor, time_minor if v_time_minor else time_major,
                wblk(0), wblk(v_colblk)]
    args = [q, kcache_t, vcache, k, vsrc]
    if mode == "fox":
        in_specs += [pl.BlockSpec((None, R, 1), lambda b, s: (b, 0, 0)),
                     pl.BlockSpec((None, nmap, tk), lambda b, s: (b, 0, jnp.minimum(s, nk - 1))),
                     pl.BlockSpec((None, nmap, LANES), lambda b, s: (b, 0, 0)),
                     wblk(gate_colblk)]
        args += [cq, ck_cache, ck_new, gate_src]
    else:
        in_specs += [pl.BlockSpec(lam.shape, lambda b, s: (0, 0)),
                     pl.BlockSpec((1, LANES), lambda b, s: (0, 0))]
        args += [lam, sub_g]
    return pl.pallas_call(
        functools.partial(_sample_attn_kernel, mode=mode, lam_init=lam_init,
                          scale=1.0 / math.sqrt(HEAD_DIM), past=past, v_time_minor=v_time_minor),
        grid=(nstream, nk + 1),
        in_specs=[pl.BlockSpec(memory_space=pl.ANY)] + in_specs,
        out_specs=pl.BlockSpec((tnew, W), lambda b, s: (rb0 + b, 0)),
        out_shape=jax.ShapeDtypeStruct(out_buf.shape, F32),
        input_output_aliases={0: 0},
        scratch_shapes=[pltpu.VMEM((R, W), BF16), pltpu.VMEM((R, 1), F32),
                        pltpu.VMEM((R, 1), F32), pltpu.VMEM((R, W), F32)],
        compiler_params=_params(("arbitrary", "arbitrary")),
        name=f"sample_attention_{mode}",
    )(out_buf, *args)


def _top16_desc(e):
    rows = []
    for _ in range(PEER_TOPK):
        m = jnp.max(e, axis=0, keepdims=True)
        rows.append(m)
        e = jnp.where(e == m, -1.0, e)
    return jnp.concatenate(rows, axis=0)


def _compare_exchange(v, i, l):
    v[i], v[l] = jnp.maximum(v[i], v[l]), jnp.minimum(v[i], v[l])


def _top16_ranks(e):
    n = e.shape[0] // SUBLANES
    assert n == PEER_TOPK
    v = [e[r * SUBLANES:(r + 1) * SUBLANES, :] for r in range(n)]
    k = 2
    while k <= n:
        j = k // 2
        while j >= 1:
            for i in range(n):
                l = i ^ j
                if l > i:
                    if i & k == 0:
                        _compare_exchange(v, i, l)
                    else:
                        _compare_exchange(v, l, i)
            j //= 2
        k *= 2
    shift = SUBLANES // 2
    while shift >= 1:
        w = [pltpu.roll(x, shift, axis=0) for x in v]
        v = [jnp.maximum(v[i], w[n - 1 - i]) for i in range(n)]
        j = n // 2
        while j >= 1:
            for i in range(n):
                l = i ^ j
                if l > i:
                    _compare_exchange(v, i, l)
            j //= 2
        shift //= 2
    return v


def _ranks_on_sublanes(v, first):
    row = lax.broadcasted_iota(jnp.int32, v[0].shape, 0)
    out = v[first + SUBLANES - 1]
    for r in range(SUBLANES - 2, -1, -1):
        out = jnp.where(row == r, v[first + r], out)
    return out


def _pair_candidates(a, b):
    row = lax.broadcasted_iota(jnp.int32, a[0].shape, 0)
    b_lo = _ranks_on_sublanes(b, 0)
    b_hi = _ranks_on_sublanes(b, SUBLANES)
    parts = [b_lo * a[0], b_hi * a[0], b_lo * a[1]]
    for r1 in range(2, SUBLANES):
        parts.append(jnp.where(row < PEER_TOPK // (r1 + 1), b_lo * a[r1], -1.0))
    parts.append(b[0] * _ranks_on_sublanes(a, SUBLANES))
    return jnp.concatenate(parts, axis=0)


def _peer_kernel(x_ref, g_ref, wq_ref, sk_ref, u0_ref, un_ref, vp_ref, vl_ref, o_ref,
                 hn_ref, e1_ref, e2_ref, thr_ref, at0_ref, at1_ref, wt0_ref, wt1_ref, yt_ref,
                 *, nheads):
    tb, D = x_ref.shape
    te = un_ref.shape[0]
    nkeys = sk_ref.shape[1]
    half = sk_ref.shape[2]
    e_step = pl.program_id(1)
    ne = pl.num_programs(1)
    assert te == SUBLANES * nkeys
    chunk_w = min(MXU_WIDTH, tb)
    nchunk = tb // chunk_w

    @pl.when(e_step == 0)
    def _():
        x = x_ref[...]
        y = x * lax.rsqrt(jnp.mean(x * x, axis=-1, keepdims=True) + EPS)
        hn_ref[...] = (y * g_ref[...]).astype(BF16)

        def head_body(h, carry):
            es = []
            for c in range(2):
                r0 = pl.multiple_of((h * 2 + c) * half, half)
                qs = lax.dot_general(wq_ref[pl.ds(r0, half), :], hn_ref[...], _NT,
                                     preferred_element_type=F32).astype(BF16)
                s = jnp.dot(sk_ref[h * 2 + c], qs, preferred_element_type=F32)
                es.append(jnp.exp(s - jnp.max(s, axis=0, keepdims=True)))
            e2_ref[h] = es[1]
            tiny = jnp.finfo(F32).tiny
            for tc in range(tb // LANES):
                sl = slice(tc * LANES, (tc + 1) * LANES)
                e1 = es[0][:, sl]
                a = _top16_ranks(e1)
                b = _top16_ranks(es[1][:, sl])
                cand = _pair_candidates(a, b)
                top = jnp.maximum(_top16_desc(cand), 0.0)
                rz = 1.0 / jnp.sum(top, axis=0, keepdims=True)
                winner = cand >= jnp.maximum(top[PEER_TOPK - 1:PEER_TOPK, :], tiny)
                cand_n = _pair_candidates([ar * rz for ar in a], b)
                thr = jnp.min(jnp.where(winner, cand_n, 2.0), axis=0, keepdims=True)
                e1_ref[h, :, sl] = e1 * rz
                thr_ref[h, :, sl] = jnp.broadcast_to(jnp.maximum(thr, tiny), (SUBLANES, LANES))
            return carry

        lax.fori_loop(0, nheads, head_body, 0)
        at0_ref[...] = lax.dot_general(u0_ref[...], hn_ref[...], _NT, preferred_element_type=F32)
        wt1_ref[...] = jnp.zeros(wt1_ref.shape, BF16)
        yt_ref[...] = jnp.zeros(yt_ref.shape, F32)

    pieces = ([("pre", r, c) for c in range(nchunk) for r in range(te // PIECE_ROWS)]
              + [("out", r, c) for c in range(nchunk) for r in range(D // PIECE_ROWS)])

    def mxu_piece(piece, at_nxt, wt_prv):
        kind, r, c = piece
        rows = slice(r * PIECE_ROWS, (r + 1) * PIECE_ROWS)
        cols = slice(c * chunk_w, (c + 1) * chunk_w)
        if kind == "pre":
            at_nxt[rows, cols] = lax.dot_general(un_ref[rows, :], hn_ref[cols, :], _NT,
                                                 preferred_element_type=F32)
        else:
            yt_ref[rows, cols] += jnp.dot(vp_ref[rows, :], wt_prv[:, cols],
                                          preferred_element_type=F32)

    def gate_unit(tc, ii, e1_tiles, at_cur, wt_cur):
        sl = slice(tc * LANES, (tc + 1) * LANES)
        for jh in range(nkeys // GATE_ROWS):
            js = slice(jh * GATE_ROWS, (jh + 1) * GATE_ROWS)
            gate = jnp.zeros((GATE_ROWS, LANES), F32)
            for h in range(nheads):
                t = e2_ref[h, js, sl] * e1_tiles[h][ii:ii + 1, :]
                gate = gate + jnp.where(t >= thr_ref[h, 0:1, sl], t, 0.0)
            rows = slice(ii * nkeys + jh * GATE_ROWS, ii * nkeys + (jh + 1) * GATE_ROWS)
            a = at_cur[rows, sl]
            act = 0.5 * a * (1.0 + lax.erf(a * (1.0 / math.sqrt(2.0))))
            wt_cur[rows, sl] = (gate * act).astype(BF16)

    def body(at_cur, at_nxt, wt_cur, wt_prv):
        i0 = pl.multiple_of(e_step * SUBLANES, SUBLANES)
        n_units = (tb // LANES) * SUBLANES
        done = 0
        for tc in range(tb // LANES):
            sl = slice(tc * LANES, (tc + 1) * LANES)
            e1_tiles = [e1_ref[h, pl.ds(i0, SUBLANES), sl] for h in range(nheads)]
            for ii in range(SUBLANES):
                unit = tc * SUBLANES + ii
                upto = min(len(pieces), unit * len(pieces) // n_units + 1)
                for piece in pieces[done:upto]:
                    mxu_piece(piece, at_nxt, wt_prv)
                done = upto
                gate_unit(tc, ii, e1_tiles, at_cur, wt_cur)

    @pl.when(e_step % 2 == 0)
    def _():
        body(at0_ref, at1_ref, wt0_ref, wt1_ref)

    @pl.when(e_step % 2 == 1)
    def _():
        body(at1_ref, at0_ref, wt1_ref, wt0_ref)

    @pl.when(e_step == ne - 1)
    def _():
        yt = yt_ref[...] + jnp.dot(vl_ref[...], wt1_ref[...], preferred_element_type=F32)
        o_ref[...] = x_ref[...] + yt.T


def peer_layer(x, g, layer, wq_t, sub_keys, u, v_t, tb):
    M, D = x.shape
    E = u.shape[1]
    nheads = sub_keys.shape[1] // 2
    nkeys = sub_keys.shape[2]
    te = SUBLANES * nkeys
    ne = E // te
    assert ne % 2 == 0
    return pl.pallas_call(
        functools.partial(_peer_kernel, nheads=nheads),
        grid=(M // tb, ne),
        in_specs=[pl.BlockSpec((tb, D), lambda i, e: (i, 0)),
                  pl.BlockSpec((1, D), lambda i, e: (0, 0)),
                  pl.BlockSpec((None,) + wq_t.shape[1:], lambda i, e: (layer, 0, 0)),
                  pl.BlockSpec((None,) + sub_keys.shape[1:], lambda i, e: (layer, 0, 0, 0)),
                  pl.BlockSpec((None, te, D), lambda i, e: (layer, 0, 0)),
                  pl.BlockSpec((None, te, D),
                               lambda i, e: (layer, jnp.minimum(e + 1, ne - 1), 0)),
                  pl.BlockSpec((None, D, te), lambda i, e: (layer, 0, jnp.maximum(e - 1, 0))),
                  pl.BlockSpec((None, D, te), lambda i, e: (layer, 0, ne - 1))],
        out_specs=pl.BlockSpec((tb, D), lambda i, e: (i, 0)),
        out_shape=jax.ShapeDtypeStruct((M, D), F32),
        scratch_shapes=[pltpu.VMEM((tb, D), BF16),
                        pltpu.VMEM((nheads, nkeys, tb), F32),
                        pltpu.VMEM((nheads, nkeys, tb), F32),
                        pltpu.VMEM((nheads, SUBLANES, tb), F32),
                        pltpu.VMEM((te, tb), F32),
                        pltpu.VMEM((te, tb), F32),
                        pltpu.VMEM((te, tb), BF16),
                        pltpu.VMEM((te, tb), BF16),
                        pltpu.VMEM((D, tb), F32)],
        compiler_params=_params(("arbitrary", "arbitrary")),
        name="peer",
    )(x, g, wq_t, sub_keys, u, u, v_t, v_t)


def _rope_tables(pos):
    inv = ROPE_THETA ** (-jnp.arange(0, HEAD_DIM, 2, dtype=F32) / HEAD_DIM)
    ang = pos.astype(F32)[:, None] * inv[None, :]
    ang = jnp.concatenate([ang, ang, ang, ang], axis=-1)
    return jnp.cos(ang), jnp.sin(ang)


def _tile_heads(g, width):
    return jnp.tile(g, width // g.shape[0]).reshape(1, width)


def _time_minor(cache):
    nd = cache.ndim
    c = jnp.transpose(cache, (0, 1) + tuple(range(3, nd)) + (2,))
    return c.reshape(c.shape[0] * c.shape[1], -1, c.shape[-1])


def _forward(x_prompt, x_sample, cache_fox_k, cache_fox_v, cache_fox_logf, cache_diff_k,
             cache_diff_v, norm_mix_g, norm_ffn_g, fox_w_in, fox_b_f, fox_q_g, fox_k_g, fox_w_o,
             diff_w_in, diff_q_g, diff_k_g, diff_lambda, diff_sub_g, diff_w_o,
             peer_w_q, peer_sub_keys, peer_u, peer_v, *, tm, tq, tk_cache, tb):
    B, S, D = x_prompt.shape
    NS, T, _ = x_sample.shape
    depth = norm_mix_g.shape[0]
    past = cache_fox_k.shape[2]
    MP = B * S
    M = MP + NS * T
    fox_heads = cache_fox_k.shape[3]
    diff_heads = cache_diff_k.shape[3]
    W = fox_heads * HEAD_DIM
    npair = W // LANES

    x = jnp.concatenate([x_prompt.reshape(MP, D), x_sample.reshape(NS * T, D)], axis=0)

    pos = jnp.concatenate([jnp.tile(jnp.arange(S, dtype=jnp.int32), B),
                           jnp.tile(past + jnp.arange(T, dtype=jnp.int32), NS)])
    cos, sin = _rope_tables(pos)

    kc_fox = _time_minor(cache_fox_k)
    vc_fox = _time_minor(cache_fox_v)
    kc_diff = _time_minor(cache_diff_k)
    vc_diff = cache_diff_v.reshape((-1, past) + cache_diff_v.shape[3:])
    lf_cache = jnp.swapaxes(cache_fox_logf, 2, 3)

    wq_t = jnp.swapaxes(peer_w_q, 1, 2).astype(BF16)
    sk_all = peer_sub_keys.reshape((depth, -1) + peer_sub_keys.shape[3:]).astype(BF16)
    u_all = peer_u.astype(BF16)
    v_t_all = jnp.swapaxes(peer_v, 1, 2).astype(BF16)

    n_fox, n_diff = cache_fox_k.shape[0], cache_diff_k.shape[0]
    outs = {k: [] for k in ("fl_p", "fk_s", "fv_s", "fl_s", "dk_s", "dv_s")}
    fk_buf = fv_buf = dk_buf = dv_buf = None

    for layer in range(depth):
        j = layer // 2
        g_mix = norm_mix_g[layer].reshape(1, D)
        if layer % 2 == 0:
            w_in = jnp.pad(fox_w_in[j], ((0, 0), (0, LANES - fox_heads))).astype(BF16)
            p = norm_matmul(x, g_mix, w_in, tm, w_in.shape[1])
            b_f = jnp.pad(fox_b_f[j], (0, LANES - fox_heads)).reshape(1, LANES)
            qn, kn, lf, lf_t = fox_post(p, b_f, _tile_heads(fox_q_g[j], W),
                                        _tile_heads(fox_k_g[j], W), tm)
            lf = lf[:, :fox_heads]
            lf_p = lf[:MP].reshape(B, S, fox_heads)
            lf_s = lf[MP:].reshape(NS, T, fox_heads)
            c_t = cumsum_lanes(lf_t[:, :MP], LANES, S)[:fox_heads]
            c_pt = jnp.swapaxes(c_t.reshape(fox_heads, B, S), 0, 1)
            c_rows = (c_pt * LOG2E).reshape(B, npair, 2, S)
            c_cols = jnp.swapaxes(c_rows, 2, 3)
            o_p = prompt_attention("fox", qn, kn, p, 2 * npair, B, S, tq,
                                   c_rows=c_rows, c_cols=c_cols, gate_src=p,
                                   gate_col0=3 * npair)
            c_cache = cumsum_lanes(lf_cache[j].reshape(NS * fox_heads, past), LANES, past)
            lf_new = jnp.pad(jnp.swapaxes(lf_s, 1, 2), ((0, 0), (0, 0), (0, LANES - T)))
            c_new = cumsum_lanes(lf_new.reshape(NS * fox_heads, LANES), LANES, LANES)
            ck_new = (c_cache[:, past - 1:] + c_new).reshape(NS, fox_heads, LANES)
            cq_s = ck_new[:, :, :T].reshape(NS, fox_heads * T, 1)
            o = sample_attention("fox", o_p, qn, kn, p, 2, kc_fox, vc_fox, True, j, NS, T, MP,
                                 tk_cache, cq=cq_s,
                                 ck_cache=c_cache.reshape(NS, fox_heads, past), ck_new=ck_new,
                                 gate_src=p, gate_colblk=3)
            x = matmul_residual(o, fox_w_o[j].astype(BF16), x, tm)
            fk_buf = emit_prompt_rows(kn, 0, W, j, n_fox, B, S, tm, True, fk_buf)
            fv_buf = emit_prompt_rows(p, 2, W, j, n_fox, B, S, tm, True, fv_buf)
            outs["fl_p"].append(lf_p)
            outs["fk_s"].append(kn[MP:].reshape(NS, T, fox_heads, HEAD_DIM))
            outs["fv_s"].append(p[MP:, 2 * W:3 * W].reshape(NS, T, fox_heads, HEAD_DIM))
            outs["fl_s"].append(lf_s)
        else:
            lam_init = 0.8 - 0.6 * math.exp(-0.3 * layer)
            p = norm_matmul(x, g_mix, diff_w_in[j].astype(BF16), tm, 3 * W)
            qn, kn = diff_post(p, cos, sin, _tile_heads(diff_q_g[j], W),
                               _tile_heads(diff_k_g[j], W), tm)
            sub_g = diff_sub_g[j].reshape(1, LANES)
            o_p = prompt_attention("diff", qn, kn, p, 2 * npair, B, S, tq,
                                   lam=diff_lambda[j], sub_g=sub_g, lam_init=lam_init)
            o = sample_attention("diff", o_p, qn, kn, p, 2, kc_diff, vc_diff, False, j, NS, T,
                                 MP, tk_cache, lam=diff_lambda[j], sub_g=sub_g,
                                 lam_init=lam_init)
            x = matmul_residual(o, diff_w_o[j].astype(BF16), x, tm)
            dk_buf = emit_prompt_rows(kn, 0, W, j, n_diff, B, S, tm, True, dk_buf)
            dv_buf = emit_prompt_rows(p, 2, W, j, n_diff, B, S, tm, False, dv_buf)
            outs["dk_s"].append(kn[MP:].reshape(NS, T, diff_heads, 2, HEAD_DIM))
            outs["dv_s"].append(p[MP:, 2 * W:3 * W].reshape(NS, T, diff_heads, 2 * HEAD_DIM))

        x = peer_layer(x, norm_ffn_g[layer].reshape(1, D), layer, wq_t, sk_all, u_all, v_t_all,
                       tb)

    fk_p = jnp.transpose(fk_buf.reshape(n_fox, B, fox_heads, HEAD_DIM, S), (0, 1, 4, 2, 3))
    fv_p = jnp.transpose(fv_buf.reshape(n_fox, B, fox_heads, HEAD_DIM, S), (0, 1, 4, 2, 3))
    dk_p = jnp.transpose(dk_buf.reshape(n_diff, B, diff_heads, 2, HEAD_DIM, S),
                         (0, 1, 5, 2, 3, 4))
    dv_p = dv_buf.reshape(n_diff, B, S, diff_heads, 2 * HEAD_DIM)
    return (x[:MP].reshape(B, S, D), x[MP:].reshape(NS, T, D),
            fk_p, fv_p, jnp.stack(outs["fl_p"]), dk_p, dv_p,
            jnp.stack(outs["fk_s"]), jnp.stack(outs["fv_s"]), jnp.stack(outs["fl_s"]),
            jnp.stack(outs["dk_s"]), jnp.stack(outs["dv_s"]))


def kernel(x_prompt, x_sample, cache_fox_k, cache_fox_v, cache_fox_logf, cache_diff_k, cache_diff_v, norm_mix_g, norm_ffn_g, fox_w_in, fox_b_f, fox_q_g, fox_k_g, fox_w_o, diff_w_in, diff_q_g, diff_k_g, diff_lambda, diff_sub_g, diff_w_o, peer_w_q, peer_sub_keys, peer_u, peer_v):
    return _forward(x_prompt, x_sample, cache_fox_k, cache_fox_v, cache_fox_logf, cache_diff_k,
                    cache_diff_v, norm_mix_g, norm_ffn_g, fox_w_in, fox_b_f, fox_q_g, fox_k_g,
                    fox_w_o, diff_w_in, diff_q_g, diff_k_g, diff_lambda, diff_sub_g, diff_w_o,
                    peer_w_q, peer_sub_keys, peer_u, peer_v,
                    tm=512, tq=512, tk_cache=1024, tb=768)
```

```python
import functools
import math

import jax
import jax.numpy as jnp
from jax import lax
from jax.experimental import pallas as pl
from jax.experimental.pallas import tpu as pltpu

F32 = jnp.float32
BF16 = jnp.bfloat16
EPS = 1e-6
NEG_INF = -1e30
CHUNK = 64
ROPE_THETA = 10000.0
LOG2E = math.log2(math.e)
PEER_TOPK = 16
LANES = 128
SUBLANES = 8
MXU_WIDTH = 256
GATE_ROWS = 32
PIECE_ROWS = 1024
PAIRS_PER_STEP = 8
HEAD_DIM = 64
VMEM_LIMIT = 56 * 1024 * 1024

_NT = (((1,), (1,)), ((), ()))


def _params(sem, vmem=VMEM_LIMIT):
    return pltpu.CompilerParams(dimension_semantics=sem, vmem_limit_bytes=vmem)


def _norm_matmul_kernel(x_ref, g_ref, w_ref, o_ref, hn_ref):
    @pl.when(pl.program_id(1) == 0)
    def _():
        x = x_ref[...]
        y = x * lax.rsqrt(jnp.mean(x * x, axis=-1, keepdims=True) + EPS)
        hn_ref[...] = (y * g_ref[...]).astype(BF16)

    o_ref[...] = jnp.dot(hn_ref[...], w_ref[...], preferred_element_type=F32)


def norm_matmul(x, g, w, tm, tn):
    M, D = x.shape
    N = w.shape[1]
    return pl.pallas_call(
        _norm_matmul_kernel,
        grid=(M // tm, N // tn),
        in_specs=[pl.BlockSpec((tm, D), lambda i, j: (i, 0)),
                  pl.BlockSpec((1, D), lambda i, j: (0, 0)),
                  pl.BlockSpec((D, tn), lambda i, j: (0, j))],
        out_specs=pl.BlockSpec((tm, tn), lambda i, j: (i, j)),
        out_shape=jax.ShapeDtypeStruct((M, N), F32),
        scratch_shapes=[pltpu.VMEM((tm, D), BF16)],
        compiler_params=_params(("arbitrary", "arbitrary")),
        name="norm_matmul",
    )(x, g, w)


def _matmul_res_kernel(a_ref, w_ref, r_ref, o_ref):
    o_ref[...] = r_ref[...] + jnp.dot(a_ref[...].astype(BF16), w_ref[...],
                                      preferred_element_type=F32)


def matmul_residual(a, w, res, tm):
    M, K = a.shape
    N = w.shape[1]
    return pl.pallas_call(
        _matmul_res_kernel,
        grid=(M // tm,),
        in_specs=[pl.BlockSpec((tm, K), lambda i: (i, 0)),
                  pl.BlockSpec((K, N), lambda i: (0, 0)),
                  pl.BlockSpec((tm, N), lambda i: (i, 0))],
        out_specs=pl.BlockSpec((tm, N), lambda i: (i, 0)),
        out_shape=jax.ShapeDtypeStruct((M, N), F32),
        compiler_params=_params(("arbitrary",)),
        name="matmul_residual",
    )(a, w, res)


def _emit_kernel(src_ref, *rest, transpose):
    o_ref = rest[-1]
    o_ref[...] = src_ref[...].T if transpose else src_ref[...]


def emit_prompt_rows(src, colblk, width, layer, nlayers, batch, seq, tm, transpose, buf=None):
    nblk = seq // tm
    in_specs = [pl.BlockSpec((tm, width), lambda b, s: (b * nblk + s, colblk))]
    args = [src]
    if buf is not None:
        in_specs.append(pl.BlockSpec(memory_space=pl.ANY))
        args.append(buf)
    if transpose:
        out_shape = (nlayers, batch, width, seq)
        out_spec = pl.BlockSpec((None, None, width, tm), lambda b, s: (layer, b, 0, s))
    else:
        out_shape = (nlayers, batch, seq, width)
        out_spec = pl.BlockSpec((None, None, tm, width), lambda b, s: (layer, b, s, 0))
    return pl.pallas_call(
        functools.partial(_emit_kernel, transpose=transpose),
        grid=(batch, nblk),
        in_specs=in_specs,
        out_specs=out_spec,
        out_shape=jax.ShapeDtypeStruct(out_shape, F32),
        input_output_aliases={} if buf is None else {1: 0},
        compiler_params=_params(("arbitrary", "arbitrary")),
        name="emit_prompt_rows",
    )(*args)


def _head_rms64(xc, lo):
    sq = xc * xc
    s_lo = jnp.sum(jnp.where(lo, sq, 0.0), axis=-1, keepdims=True)
    s_hi = jnp.sum(jnp.where(lo, 0.0, sq), axis=-1, keepdims=True)
    ms = jnp.where(lo, s_lo, s_hi) * (1.0 / HEAD_DIM)
    return lax.rsqrt(ms + EPS)


def _fox_post_kernel(q_ref, k_ref, f_ref, bf_ref, qg_ref, kg_ref, qo_ref, ko_ref, lf_ref,
                     lft_ref):
    tm, width = q_ref.shape
    lo = lax.broadcasted_iota(jnp.int32, (tm, LANES), 1) < HEAD_DIM
    for src, g_ref, dst in ((q_ref, qg_ref, qo_ref), (k_ref, kg_ref, ko_ref)):
        for c in range(width // LANES):
            sl = slice(c * LANES, (c + 1) * LANES)
            xc = src[:, sl]
            dst[:, sl] = xc * _head_rms64(xc, lo) * g_ref[:, sl]
    z = f_ref[...] + bf_ref[...]
    lf = jnp.minimum(z, 0.0) - jnp.log1p(jnp.exp(-jnp.abs(z)))
    lf_ref[...] = lf
    lft_ref[...] = lf.T


def fox_post(p, b_f, q_g, k_g, tm):
    M = p.shape[0]
    W = q_g.shape[1]
    row = lambda c: pl.BlockSpec((tm, W), lambda i, c=c: (i, c))
    vec = lambda n: pl.BlockSpec((1, n), lambda i: (0, 0))
    return pl.pallas_call(
        _fox_post_kernel,
        grid=(M // tm,),
        in_specs=[row(0), row(1), pl.BlockSpec((tm, LANES), lambda i: (i, 4 * W // LANES)),
                  vec(LANES), vec(W), vec(W)],
        out_specs=[row(0), row(0), pl.BlockSpec((tm, LANES), lambda i: (i, 0)),
                   pl.BlockSpec((LANES, tm), lambda i: (0, i))],
        out_shape=[jax.ShapeDtypeStruct((M, W), F32), jax.ShapeDtypeStruct((M, W), F32),
                   jax.ShapeDtypeStruct((M, LANES), F32), jax.ShapeDtypeStruct((LANES, M), F32)],
        compiler_params=_params(("arbitrary",)),
        name="fox_post",
    )(p, p, p, b_f, q_g, k_g)


def _diff_post_kernel(q_ref, k_ref, cos_ref, sin_ref, qg_ref, kg_ref, qo_ref, ko_ref):
    tm, width = q_ref.shape
    lane = lax.broadcasted_iota(jnp.int32, (tm, LANES), 1)
    lo = lane < HEAD_DIM
    first_half = (lane % HEAD_DIM) < (HEAD_DIM // 2)
    cos = cos_ref[...]
    sin = sin_ref[...]
    for src, g_ref, dst in ((q_ref, qg_ref, qo_ref), (k_ref, kg_ref, ko_ref)):
        for c in range(width // LANES):
            sl = slice(c * LANES, (c + 1) * LANES)
            xc = src[:, sl]
            xn = xc * _head_rms64(xc, lo) * g_ref[:, sl]
            rot = jnp.where(first_half,
                            -pltpu.roll(xn, LANES - HEAD_DIM // 2, axis=1),
                            pltpu.roll(xn, HEAD_DIM // 2, axis=1))
            dst[:, sl] = xn * cos + rot * sin


def diff_post(p, cos, sin, q_g, k_g, tm):
    M = p.shape[0]
    W = q_g.shape[1]
    row = lambda c: pl.BlockSpec((tm, W), lambda i, c=c: (i, c))
    tab = pl.BlockSpec((tm, LANES), lambda i: (i, 0))
    vec = pl.BlockSpec((1, W), lambda i: (0, 0))
    return pl.pallas_call(
        _diff_post_kernel,
        grid=(M // tm,),
        in_specs=[row(0), row(1), tab, tab, vec, vec],
        out_specs=[row(0), row(0)],
        out_shape=[jax.ShapeDtypeStruct((M, W), F32), jax.ShapeDtypeStruct((M, W), F32)],
        compiler_params=_params(("arbitrary",)),
        name="diff_post",
    )(p, p, cos, sin, q_g, k_g)


def _cumsum_kernel(x_ref, o_ref):
    rows, length = x_ref.shape
    s_idx = lax.broadcasted_iota(jnp.int32, (LANES, LANES), 0)
    t_idx = lax.broadcasted_iota(jnp.int32, (LANES, LANES), 1)
    tri = (s_idx <= t_idx).astype(F32)
    carry = jnp.zeros((rows, 1), F32)
    for blk in range(length // LANES):
        sl = slice(blk * LANES, (blk + 1) * LANES)
        c = jnp.dot(x_ref[:, sl], tri, preferred_element_type=F32,
                    precision=lax.Precision.HIGHEST) + carry
        o_ref[:, sl] = c
        carry = c[:, LANES - 1:LANES]


def cumsum_lanes(x, rows, length):
    R, L = x.shape
    return pl.pallas_call(
        _cumsum_kernel,
        grid=(R // rows, L // length),
        in_specs=[pl.BlockSpec((rows, length), lambda i, j: (i, j))],
        out_specs=pl.BlockSpec((rows, length), lambda i, j: (i, j)),
        out_shape=jax.ShapeDtypeStruct((R, L), F32),
        compiler_params=_params(("arbitrary", "arbitrary")),
        name="cumsum_lanes",
    )(x)


def _diff_lambda(lam_ref, lam_init):
    lf = lam_ref[...]
    a = jnp.sum(lf[0:1, :] * lf[1:2, :], axis=-1, keepdims=True)
    b = jnp.sum(lf[2:3, :] * lf[3:4, :], axis=-1, keepdims=True)
    return jnp.exp(a) - jnp.exp(b) + lam_init


def _prompt_attn_kernel(qi_tab, ki_tab, q_ref, k_ref, v_ref, *rest, mode, lam_init, scale):
    if mode == "fox":
        cq_ref, ck_ref, gate_ref, o_ref = rest[:4]
        scratch = rest[4:]
    else:
        lam_ref, subg_ref, o_ref = rest[:3]
        scratch = rest[3:]
    qs_ref, m_ref, l_ref, acc_ref = scratch
    tq = q_ref.shape[0]
    tk = k_ref.shape[0]
    pairs = q_ref.shape[1] // LANES
    t = pl.program_id(2)
    qi = qi_tab[t]
    ki = ki_tab[t]

    @pl.when(ki == 0)
    def _():
        qs_ref[...] = (q_ref[...] * scale).astype(BF16)
        m_ref[...] = jnp.full(m_ref.shape, NEG_INF, F32)
        l_ref[...] = jnp.zeros(l_ref.shape, F32)
        acc_ref[...] = jnp.zeros(acc_ref.shape, F32)

    def update(s, vt, pp, mp):
        m_prev = m_ref[pp, mp]
        m_new = jnp.maximum(m_prev, jnp.max(s, axis=0, keepdims=True))
        alpha = jnp.exp2(m_prev - m_new)
        p = jnp.exp2(s - m_new)
        l_ref[pp, mp] = alpha * l_ref[pp, mp] + jnp.sum(p, axis=0, keepdims=True)
        acc_ref[pp, mp] = alpha * acc_ref[pp, mp] + jnp.dot(vt, p.astype(BF16),
                                                            preferred_element_type=F32)
        m_ref[pp, mp] = m_new

    def step(pp, diagonal):
        sl = slice(pp * LANES, (pp + 1) * LANES)
        k = k_ref[:, sl]
        lo_k = lax.broadcasted_iota(jnp.int32, (tk, LANES), 1) < HEAD_DIM
        k0 = jnp.where(lo_k, k, 0.0).astype(BF16)
        k1 = jnp.where(lo_k, 0.0, k).astype(BF16)
        qs = qs_ref[:, sl]
        s0 = lax.dot_general(k0, qs, _NT, preferred_element_type=F32)
        s1 = lax.dot_general(k1, qs, _NT, preferred_element_type=F32)
        if mode == "fox":
            s0 = s0 + cq_ref[pp, 0:1, :] - ck_ref[pp, :, 0:1]
            s1 = s1 + cq_ref[pp, 1:2, :] - ck_ref[pp, :, 1:2]
        if diagonal:
            key = lax.broadcasted_iota(jnp.int32, (tk, tq), 0)
            qry = lax.broadcasted_iota(jnp.int32, (tk, tq), 1)
            if mode == "fox":
                visible = key <= qry
            else:
                visible = (key // CHUNK) <= (qry // CHUNK)
            s0 = jnp.where(visible, s0, NEG_INF)
            s1 = jnp.where(visible, s1, NEG_INF)
        vt = v_ref[:, sl].T.astype(BF16)
        update(s0, vt, pp, 0)
        update(s1, vt, pp, 1)

    def finish(pp):
        sl = slice(pp * LANES, (pp + 1) * LANES)
        o0 = acc_ref[pp, 0] / l_ref[pp, 0]
        o1 = acc_ref[pp, 1] / l_ref[pp, 1]
        if mode == "fox":
            first_head = lax.broadcasted_iota(jnp.int32, (LANES, tq), 0) < HEAD_DIM
            ot = jnp.where(first_head, o0, o1)
            o_ref[:, sl] = ot.T * jax.nn.sigmoid(gate_ref[:, sl])
        else:
            ot = o0 - _diff_lambda(lam_ref, lam_init) * o1
            y = ot * lax.rsqrt(jnp.mean(ot * ot, axis=0, keepdims=True) + EPS)
            o_ref[:, sl] = (y.T * subg_ref[...]) * (1.0 - lam_init)

    @pl.when(ki < qi)
    def _():
        for pp in range(pairs):
            step(pp, False)

    @pl.when(ki == qi)
    def _():
        for pp in range(pairs):
            step(pp, True)
            finish(pp)


def prompt_attention(mode, q, k, vsrc, v_col0, batch, seq, tq, *, c_rows=None, c_cols=None,
                     gate_src=None, gate_col0=None, lam=None, sub_g=None, lam_init=0.0):
    W = q.shape[1]
    npair = W // LANES
    nq = seq // tq
    tri = [(a, b) for a in range(nq) for b in range(a + 1)]
    qi_tab = jnp.asarray([a for a, _ in tri], jnp.int32)
    ki_tab = jnp.asarray([b for _, b in tri], jnp.int32)
    pp = math.gcd(PAIRS_PER_STEP, npair)
    assert v_col0 % pp == 0 and (gate_col0 or 0) % pp == 0
    wide = pp * LANES
    qmap = lambda b, h, t, qt, kt: (b * nq + qt[t], h)
    kmap = lambda b, h, t, qt, kt: (b * nq + kt[t], h)
    in_specs = [pl.BlockSpec((tq, wide), qmap),
                pl.BlockSpec((tq, wide), kmap),
                pl.BlockSpec((tq, wide),
                             lambda b, h, t, qt, kt: (b * nq + kt[t], v_col0 // pp + h))]
    args = [q, k, vsrc]
    if mode == "fox":
        in_specs += [pl.BlockSpec((None, pp, 2, tq), lambda b, h, t, qt, kt: (b, h, 0, qt[t])),
                     pl.BlockSpec((None, pp, tq, 2), lambda b, h, t, qt, kt: (b, h, kt[t], 0)),
                     pl.BlockSpec((tq, wide),
                                  lambda b, h, t, qt, kt: (b * nq + qt[t], gate_col0 // pp + h))]
        args += [c_rows, c_cols, gate_src]
    else:
        in_specs += [pl.BlockSpec(lam.shape, lambda b, h, t, qt, kt: (0, 0)),
                     pl.BlockSpec((1, LANES), lambda b, h, t, qt, kt: (0, 0))]
        args += [lam, sub_g]
    grid_spec = pltpu.PrefetchScalarGridSpec(
        num_scalar_prefetch=2,
        grid=(batch, npair // pp, len(tri)),
        in_specs=in_specs,
        out_specs=pl.BlockSpec((tq, wide), qmap),
        scratch_shapes=[pltpu.VMEM((tq, wide), BF16),
                        pltpu.VMEM((pp, 2, 1, tq), F32), pltpu.VMEM((pp, 2, 1, tq), F32),
                        pltpu.VMEM((pp, 2, LANES, tq), F32)],
    )
    return pl.pallas_call(
        functools.partial(_prompt_attn_kernel, mode=mode, lam_init=lam_init,
                          scale=LOG2E / math.sqrt(HEAD_DIM)),
        grid_spec=grid_spec,
        out_shape=jax.ShapeDtypeStruct(q.shape, F32),
        compiler_params=_params(("arbitrary", "arbitrary", "arbitrary")),
        name=f"prompt_attention_{mode}",
    )(qi_tab, ki_tab, *args)


def _sample_attn_kernel(buf_ref, q_ref, kc_ref, vc_ref, kn_ref, vn_ref, *rest, mode, lam_init,
                        scale, past, v_time_minor):
    del buf_ref
    if mode == "fox":
        cq_ref, ckc_ref, ckn_ref, gate_ref, o_ref = rest[:5]
        scratch = rest[5:]
    else:
        lam_ref, subg_ref, o_ref = rest[:3]
        scratch = rest[3:]
    qbd_ref, m_ref, l_ref, acc_ref = scratch
    tnew, W = q_ref.shape
    nmap = W // HEAD_DIM
    R = nmap * tnew
    kstep = pl.program_id(1)
    nk = pl.num_programs(1) - 1

    @pl.when(kstep == 0)
    def _():
        q = q_ref[...] * scale
        qt = jnp.concatenate([q] * nmap, axis=0)
        rmap = lax.broadcasted_iota(jnp.int32, (R, W), 0) // tnew
        cmap = lax.broadcasted_iota(jnp.int32, (R, W), 1) // HEAD_DIM
        qbd_ref[...] = jnp.where(rmap == cmap, qt, 0.0).astype(BF16)
        m_ref[...] = jnp.full(m_ref.shape, NEG_INF, F32)
        l_ref[...] = jnp.zeros(l_ref.shape, F32)
        acc_ref[...] = jnp.zeros(acc_ref.shape, F32)

    def expand_rows(c):
        n = c.shape[1]
        return jnp.broadcast_to(c[:, None, :], (nmap, tnew, n)).reshape(R, n)

    def update(s, pv):
        m_prev = m_ref[...]
        m_new = jnp.maximum(m_prev, jnp.max(s, axis=-1, keepdims=True))
        alpha = jnp.exp(m_prev - m_new)
        p = jnp.exp(s - m_new)
        l_ref[...] = alpha * l_ref[...] + jnp.sum(p, axis=-1, keepdims=True)
        acc_ref[...] = alpha * acc_ref[...] + pv(p.astype(BF16))
        m_ref[...] = m_new

    @pl.when(kstep < nk)
    def _():
        s = jnp.dot(qbd_ref[...], kc_ref[...].astype(BF16), preferred_element_type=F32)
        if mode == "fox":
            s = s + cq_ref[...] - expand_rows(ckc_ref[...])
        if v_time_minor:
            v = vc_ref[...].astype(BF16)
            update(s, lambda p: lax.dot_general(p, v, _NT, preferred_element_type=F32))
        else:
            v = jnp.concatenate([vc_ref[:, h, :] for h in range(vc_ref.shape[1])],
                                axis=1).astype(BF16)
            update(s, lambda p: jnp.dot(p, v, preferred_element_type=F32))

    @pl.when(kstep == nk)
    def _():
        npad = LANES - tnew
        k = jnp.concatenate([kn_ref[...], jnp.zeros((npad, W), F32)], axis=0).astype(BF16)
        v = jnp.concatenate([vn_ref[...], jnp.zeros((npad, W), F32)], axis=0).astype(BF16)
        s = lax.dot_general(qbd_ref[...], k, _NT, preferred_element_type=F32)
        if mode == "fox":
            s = s + cq_ref[...] - expand_rows(ckn_ref[...])
        tpos = lax.broadcasted_iota(jnp.int32, (R, LANES), 0) % tnew
        spos = lax.broadcasted_iota(jnp.int32, (R, LANES), 1)
        if mode == "fox":
            visible = spos <= tpos
        else:
            visible = ((past + spos) // CHUNK) <= ((past + tpos) // CHUNK)
        visible = jnp.logical_and(visible, spos < tnew)
        update(jnp.where(visible, s, NEG_INF),
               lambda p: jnp.dot(p, v, preferred_element_type=F32))

        accn = acc_ref[...] / l_ref[...]
        if mode == "fox":
            lane_map = lax.broadcasted_iota(jnp.int32, (tnew, W), 1) // HEAD_DIM
            o = jnp.zeros((tnew, W), F32)
            for r in range(nmap):
                o = jnp.where(lane_map == r, accn[r * tnew:(r + 1) * tnew, :], o)
            o_ref[...] = o * jax.nn.sigmoid(gate_ref[...])
        else:
            lam = _diff_lambda(lam_ref, lam_init)
            for h in range(nmap // 2):
                sl = slice(h * LANES, (h + 1) * LANES)
                o = (accn[(2 * h) * tnew:(2 * h + 1) * tnew, sl]
                     - lam * accn[(2 * h + 1) * tnew:(2 * h + 2) * tnew, sl])
                y = o * lax.rsqrt(jnp.mean(o * o, axis=-1, keepdims=True) + EPS)
                o_ref[:, sl] = (y * subg_ref[...]) * (1.0 - lam_init)


def sample_attention(mode, out_buf, q, k, vsrc, v_colblk, kcache_t, vcache, v_time_minor,
                     layer_idx, nstream, tnew, row0, tk, *, cq=None, ck_cache=None, ck_new=None,
                     gate_src=None, gate_colblk=None, lam=None, sub_g=None, lam_init=0.0):
    W = q.shape[1]
    past = kcache_t.shape[2]
    nk = past // tk
    nmap = W // HEAD_DIM
    R = nmap * tnew
    rb0 = row0 // tnew
    wblk = lambda colblk: pl.BlockSpec((tnew, W), lambda b, s, c=colblk: (rb0 + b, c))
    time_minor = pl.BlockSpec((None, W, tk),
                              lambda b, s: (layer_idx * nstream + b, 0, jnp.minimum(s, nk - 1)))
    time_major = pl.BlockSpec((None, tk) + vcache.shape[2:],
                              lambda b, s: (layer_idx * nstream + b, jnp.minimum(s, nk - 1), 0, 0))
    in_specs = [wblk(0), time_minor, time_minor if v_time_minor else time_major,
                wblk(0), wblk(v_colblk)]
    args = [q, kcache_t, vcache, k, vsrc]
    if mode == "fox":
        in_specs += [pl.BlockSpec((None, R, 1), lambda b, s: (b, 0, 0)),
                     pl.BlockSpec((None, nmap, tk), lambda b, s: (b, 0, jnp.minimum(s, nk - 1))),
                     pl.BlockSpec((None, nmap, LANES), lambda b, s: (b, 0, 0)),
                     wblk(gate_colblk)]
        args += [cq, ck_cache, ck_new, gate_src]
    else:
        in_specs += [pl.BlockSpec(lam.shape, lambda b, s: (0, 0)),
                     pl.BlockSpec((1, LANES), lambda b, s: (0, 0))]
        args += [lam, sub_g]
    return pl.pallas_call(
        functools.partial(_sample_attn_kernel, mode=mode, lam_init=lam_init,
                          scale=1.0 / math.sqrt(HEAD_DIM), past=past, v_time_minor=v_time_minor),
        grid=(nstream, nk + 1),
        in_specs=[pl.BlockSpec(memory_space=pl.ANY)] + in_specs,
        out_specs=pl.BlockSpec((tnew, W), lambda b, s: (rb0 + b, 0)),
        out_shape=jax.ShapeDtypeStruct(out_buf.shape, F32),
        input_output_aliases={0: 0},
        scratch_shapes=[pltpu.VMEM((R, W), BF16), pltpu.VMEM((R, 1), F32),
                        pltpu.VMEM((R, 1), F32), pltpu.VMEM((R, W), F32)],
        compiler_params=_params(("arbitrary", "arbitrary")),
        name=f"sample_attention_{mode}",
    )(out_buf, *args)


def _top16_desc(e):
    rows = []
    for _ in range(PEER_TOPK):
        m = jnp.max(e, axis=0, keepdims=True)
        rows.append(m)
        e = jnp.where(e == m, -1.0, e)
    return jnp.concatenate(rows, axis=0)


def _compare_exchange(v, i, l):
    v[i], v[l] = jnp.maximum(v[i], v[l]), jnp.minimum(v[i], v[l])


def _top16_ranks(e):
    n = e.shape[0] // SUBLANES
    assert n == PEER_TOPK
    v = [e[r * SUBLANES:(r + 1) * SUBLANES, :] for r in range(n)]
    k = 2
    while k <= n:
        j = k // 2
        while j >= 1:
            for i in range(n):
                l = i ^ j
                if l > i:
                    if i & k == 0:
                        _compare_exchange(v, i, l)
                    else:
                        _compare_exchange(v, l, i)
            j //= 2
        k *= 2
    shift = SUBLANES // 2
    while shift >= 1:
        w = [pltpu.roll(x, shift, axis=0) for x in v]
        v = [jnp.maximum(v[i], w[n - 1 - i]) for i in range(n)]
        j = n // 2
        while j >= 1:
            for i in range(n):
                l = i ^ j
                if l > i:
                    _compare_exchange(v, i, l)
            j //= 2
        shift //= 2
    return v


def _ranks_on_sublanes(v, first):
    row = lax.broadcasted_iota(jnp.int32, v[0].shape, 0)
    out = v[first + SUBLANES - 1]
    for r in range(SUBLANES - 2, -1, -1):
        out = jnp.where(row == r, v[first + r], out)
    return out


def _pair_candidates(a, b):
    row = lax.broadcasted_iota(jnp.int32, a[0].shape, 0)
    b_lo = _ranks_on_sublanes(b, 0)
    b_hi = _ranks_on_sublanes(b, SUBLANES)
    parts = [b_lo * a[0], b_hi * a[0], b_lo * a[1]]
    for r1 in range(2, SUBLANES):
        parts.append(jnp.where(row < PEER_TOPK // (r1 + 1), b_lo * a[r1], -1.0))
    parts.append(b[0] * _ranks_on_sublanes(a, SUBLANES))
    return jnp.concatenate(parts, axis=0)


def _peer_kernel(x_ref, g_ref, wq_ref, sk_ref, u0_ref, un_ref, vp_ref, vl_ref, o_ref,
                 hn_ref, e1_ref, e2_ref, thr_ref, at0_ref, at1_ref, wt0_ref, wt1_ref, yt_ref,
                 *, nheads):
    tb, D = x_ref.shape
    te = un_ref.shape[0]
    nkeys = sk_ref.shape[1]
    half = sk_ref.shape[2]
    e_step = pl.program_id(1)
    ne = pl.num_programs(1)
    assert te == SUBLANES * nkeys
    chunk_w = min(MXU_WIDTH, tb)
    nchunk = tb // chunk_w

    @pl.when(e_step == 0)
    def _():
        x = x_ref[...]
        y = x * lax.rsqrt(jnp.mean(x * x, axis=-1, keepdims=True) + EPS)
        hn_ref[...] = (y * g_ref[...]).astype(BF16)

        def head_body(h, carry):
            es = []
            for c in range(2):
                r0 = pl.multiple_of((h * 2 + c) * half, half)
                qs = lax.dot_general(wq_ref[pl.ds(r0, half), :], hn_ref[...], _NT,
                                     preferred_element_type=F32).astype(BF16)
                s = jnp.dot(sk_ref[h * 2 + c], qs, preferred_element_type=F32)
                es.append(jnp.exp(s - jnp.max(s, axis=0, keepdims=True)))
            e2_ref[h] = es[1]
            tiny = jnp.finfo(F32).tiny
            for tc in range(tb // LANES):
                sl = slice(tc * LANES, (tc + 1) * LANES)
                e1 = es[0][:, sl]
                a = _top16_ranks(e1)
                b = _top16_ranks(es[1][:, sl])
                cand = _pair_candidates(a, b)
                top = jnp.maximum(_top16_desc(cand), 0.0)
                rz = 1.0 / jnp.sum(top, axis=0, keepdims=True)
                winner = cand >= jnp.maximum(top[PEER_TOPK - 1:PEER_TOPK, :], tiny)
                cand_n = _pair_candidates([ar * rz for ar in a], b)
                thr = jnp.min(jnp.where(winner, cand_n, 2.0), axis=0, keepdims=True)
                e1_ref[h, :, sl] = e1 * rz
                thr_ref[h, :, sl] = jnp.broadcast_to(jnp.maximum(thr, tiny), (SUBLANES, LANES))
            return carry

        def two_heads(it, carry):
            head_body(2 * it, carry)
            return head_body(2 * it + 1, carry)

        assert nheads % 2 == 0
        lax.fori_loop(0, nheads // 2, two_heads, 0)
        at0_ref[...] = lax.dot_general(u0_ref[...], hn_ref[...], _NT, preferred_element_type=F32)
        wt1_ref[...] = jnp.zeros(wt1_ref.shape, BF16)
        yt_ref[...] = jnp.zeros(yt_ref.shape, F32)

    pieces = ([("pre", r, c) for c in range(nchunk) for r in range(te // PIECE_ROWS)]
              + [("out", r, c) for c in range(nchunk) for r in range(D // PIECE_ROWS)])

    def mxu_piece(piece, at_nxt, wt_prv):
        kind, r, c = piece
        rows = slice(r * PIECE_ROWS, (r + 1) * PIECE_ROWS)
        cols = slice(c * chunk_w, (c + 1) * chunk_w)
        if kind == "pre":
            at_nxt[rows, cols] = lax.dot_general(un_ref[rows, :], hn_ref[cols, :], _NT,
                                                 preferred_element_type=F32)
        else:
            yt_ref[rows, cols] += jnp.dot(vp_ref[rows, :], wt_prv[:, cols],
                                          preferred_element_type=F32)

    def gate_unit(tc, ii, e1_tiles, at_cur, wt_cur):
        sl = slice(tc * LANES, (tc + 1) * LANES)
        for jh in range(nkeys // GATE_ROWS):
            js = slice(jh * GATE_ROWS, (jh + 1) * GATE_ROWS)
            gate = jnp.zeros((GATE_ROWS, LANES), F32)
            for h in range(nheads):
                t = e2_ref[h, js, sl] * e1_tiles[h][ii:ii + 1, :]
                gate = gate + jnp.where(t >= thr_ref[h, 0:1, sl], t, 0.0)
            rows = slice(ii * nkeys + jh * GATE_ROWS, ii * nkeys + (jh + 1) * GATE_ROWS)
            a = at_cur[rows, sl]
            act = 0.5 * a * (1.0 + lax.erf(a * (1.0 / math.sqrt(2.0))))
            wt_cur[rows, sl] = (gate * act).astype(BF16)

    def body(at_cur, at_nxt, wt_cur, wt_prv):
        i0 = pl.multiple_of(e_step * SUBLANES, SUBLANES)
        n_units = (tb // LANES) * SUBLANES
        done = 0
        for tc in range(tb // LANES):
            sl = slice(tc * LANES, (tc + 1) * LANES)
            e1_tiles = [e1_ref[h, pl.ds(i0, SUBLANES), sl] for h in range(nheads)]
            for ii in range(SUBLANES):
                unit = tc * SUBLANES + ii
                upto = min(len(pieces), unit * len(pieces) // n_units + 1)
                for piece in pieces[done:upto]:
                    mxu_piece(piece, at_nxt, wt_prv)
                done = upto
                gate_unit(tc, ii, e1_tiles, at_cur, wt_cur)

    @pl.when(e_step % 2 == 0)
    def _():
        body(at0_ref, at1_ref, wt0_ref, wt1_ref)

    @pl.when(e_step % 2 == 1)
    def _():
        body(at1_ref, at0_ref, wt1_ref, wt0_ref)

    @pl.when(e_step == ne - 1)
    def _():
        yt = yt_ref[...] + jnp.dot(vl_ref[...], wt1_ref[...], preferred_element_type=F32)
        o_ref[...] = x_ref[...] + yt.T


def peer_layer(x, g, layer, wq_t, sub_keys, u, v_t, tb):
    M, D = x.shape
    E = u.shape[1]
    nheads = sub_keys.shape[1] // 2
    nkeys = sub_keys.shape[2]
    te = SUBLANES * nkeys
    ne = E // te
    assert ne % 2 == 0
    return pl.pallas_call(
        functools.partial(_peer_kernel, nheads=nheads),
        grid=(M // tb, ne),
        in_specs=[pl.BlockSpec((tb, D), lambda i, e: (i, 0)),
                  pl.BlockSpec((1, D), lambda i, e: (0, 0)),
                  pl.BlockSpec((None,) + wq_t.shape[1:], lambda i, e: (layer, 0, 0)),
                  pl.BlockSpec((None,) + sub_keys.shape[1:], lambda i, e: (layer, 0, 0, 0)),
                  pl.BlockSpec((None, te, D), lambda i, e: (layer, 0, 0)),
                  pl.BlockSpec((None, te, D),
                               lambda i, e: (layer, jnp.minimum(e + 1, ne - 1), 0)),
                  pl.BlockSpec((None, D, te), lambda i, e: (layer, 0, jnp.maximum(e - 1, 0))),
                  pl.BlockSpec((None, D, te), lambda i, e: (layer, 0, ne - 1))],
        out_specs=pl.BlockSpec((tb, D), lambda i, e: (i, 0)),
        out_shape=jax.ShapeDtypeStruct((M, D), F32),
        scratch_shapes=[pltpu.VMEM((tb, D), BF16),
                        pltpu.VMEM((nheads, nkeys, tb), F32),
                        pltpu.VMEM((nheads, nkeys, tb), F32),
                        pltpu.VMEM((nheads, SUBLANES, tb), F32),
                        pltpu.VMEM((te, tb), F32),
                        pltpu.VMEM((te, tb), F32),
                        pltpu.VMEM((te, tb), BF16),
                        pltpu.VMEM((te, tb), BF16),
                        pltpu.VMEM((D, tb), F32)],
        compiler_params=_params(("arbitrary", "arbitrary")),
        name="peer",
    )(x, g, wq_t, sub_keys, u, u, v_t, v_t)


def _rope_tables(pos):
    inv = ROPE_THETA ** (-jnp.arange(0, HEAD_DIM, 2, dtype=F32) / HEAD_DIM)
    ang = pos.astype(F32)[:, None] * inv[None, :]
    ang = jnp.concatenate([ang, ang, ang, ang], axis=-1)
    return jnp.cos(ang), jnp.sin(ang)


def _tile_heads(g, width):
    return jnp.tile(g, width // g.shape[0]).reshape(1, width)


def _time_minor(cache):
    nd = cache.ndim
    c = jnp.transpose(cache, (0, 1) + tuple(range(3, nd)) + (2,))
    return c.reshape(c.shape[0] * c.shape[1], -1, c.shape[-1])


def _forward(x_prompt, x_sample, cache_fox_k, cache_fox_v, cache_fox_logf, cache_diff_k,
             cache_diff_v, norm_mix_g, norm_ffn_g, fox_w_in, fox_b_f, fox_q_g, fox_k_g, fox_w_o,
             diff_w_in, diff_q_g, diff_k_g, diff_lambda, diff_sub_g, diff_w_o,
             peer_w_q, peer_sub_keys, peer_u, peer_v, *, tm, tq, tk_cache, tb):
    B, S, D = x_prompt.shape
    NS, T, _ = x_sample.shape
    depth = norm_mix_g.shape[0]
    past = cache_fox_k.shape[2]
    MP = B * S
    M = MP + NS * T
    fox_heads = cache_fox_k.shape[3]
    diff_heads = cache_diff_k.shape[3]
    W = fox_heads * HEAD_DIM
    npair = W // LANES

    x = jnp.concatenate([x_prompt.reshape(MP, D), x_sample.reshape(NS * T, D)], axis=0)

    pos = jnp.concatenate([jnp.tile(jnp.arange(S, dtype=jnp.int32), B),
                           jnp.tile(past + jnp.arange(T, dtype=jnp.int32), NS)])
    cos, sin = _rope_tables(pos)

    kc_fox = _time_minor(cache_fox_k)
    vc_fox = _time_minor(cache_fox_v)
    kc_diff = _time_minor(cache_diff_k)
    vc_diff = cache_diff_v.reshape((-1, past) + cache_diff_v.shape[3:])
    lf_cache = jnp.swapaxes(cache_fox_logf, 2, 3)

    wq_t = jnp.swapaxes(peer_w_q, 1, 2).astype(BF16)
    sk_all = peer_sub_keys.reshape((depth, -1) + peer_sub_keys.shape[3:]).astype(BF16)
    u_all = peer_u.astype(BF16)
    v_t_all = jnp.swapaxes(peer_v, 1, 2).astype(BF16)

    n_fox, n_diff = cache_fox_k.shape[0], cache_diff_k.shape[0]
    outs = {k: [] for k in ("fl_p", "fk_s", "fv_s", "fl_s", "dk_s", "dv_s")}
    fk_buf = fv_buf = dk_buf = dv_buf = None

    for layer in range(depth):
        j = layer // 2
        g_mix = norm_mix_g[layer].reshape(1, D)
        if layer % 2 == 0:
            w_in = jnp.pad(fox_w_in[j], ((0, 0), (0, LANES - fox_heads))).astype(BF16)
            p = norm_matmul(x, g_mix, w_in, tm, w_in.shape[1])
            b_f = jnp.pad(fox_b_f[j], (0, LANES - fox_heads)).reshape(1, LANES)
            qn, kn, lf, lf_t = fox_post(p, b_f, _tile_heads(fox_q_g[j], W),
                                        _tile_heads(fox_k_g[j], W), tm)
            lf = lf[:, :fox_heads]
            lf_p = lf[:MP].reshape(B, S, fox_heads)
            lf_s = lf[MP:].reshape(NS, T, fox_heads)
            c_t = cumsum_lanes(lf_t[:, :MP], LANES, S)[:fox_heads]
            c_pt = jnp.swapaxes(c_t.reshape(fox_heads, B, S), 0, 1)
            c_rows = (c_pt * LOG2E).reshape(B, npair, 2, S)
            c_cols = jnp.swapaxes(c_rows, 2, 3)
            o_p = prompt_attention("fox", qn, kn, p, 2 * npair, B, S, tq,
                                   c_rows=c_rows, c_cols=c_cols, gate_src=p,
                                   gate_col0=3 * npair)
            c_cache = cumsum_lanes(lf_cache[j].reshape(NS * fox_heads, past), LANES, past)
            lf_new = jnp.pad(jnp.swapaxes(lf_s, 1, 2), ((0, 0), (0, 0), (0, LANES - T)))
            c_new = cumsum_lanes(lf_new.reshape(NS * fox_heads, LANES), LANES, LANES)
            ck_new = (c_cache[:, past - 1:] + c_new).reshape(NS, fox_heads, LANES)
            cq_s = ck_new[:, :, :T].reshape(NS, fox_heads * T, 1)
            o = sample_attention("fox", o_p, qn, kn, p, 2, kc_fox, vc_fox, True, j, NS, T, MP,
                                 tk_cache, cq=cq_s,
                                 ck_cache=c_cache.reshape(NS, fox_heads, past), ck_new=ck_new,
                                 gate_src=p, gate_colblk=3)
            x = matmul_residual(o, fox_w_o[j].astype(BF16), x, tm)
            fk_buf = emit_prompt_rows(kn, 0, W, j, n_fox, B, S, tm, True, fk_buf)
            fv_buf = emit_prompt_rows(p, 2, W, j, n_fox, B, S, tm, True, fv_buf)
            outs["fl_p"].append(lf_p)
            outs["fk_s"].append(kn[MP:].reshape(NS, T, fox_heads, HEAD_DIM))
            outs["fv_s"].append(p[MP:, 2 * W:3 * W].reshape(NS, T, fox_heads, HEAD_DIM))
            outs["fl_s"].append(lf_s)
        else:
            lam_init = 0.8 - 0.6 * math.exp(-0.3 * layer)
            p = norm_matmul(x, g_mix, diff_w_in[j].astype(BF16), tm, 3 * W)
            qn, kn = diff_post(p, cos, sin, _tile_heads(diff_q_g[j], W),
                               _tile_heads(diff_k_g[j], W), tm)
            sub_g = diff_sub_g[j].reshape(1, LANES)
            o_p = prompt_attention("diff", qn, kn, p, 2 * npair, B, S, tq,
                                   lam=diff_lambda[j], sub_g=sub_g, lam_init=lam_init)
            o = sample_attention("diff", o_p, qn, kn, p, 2, kc_diff, vc_diff, False, j, NS, T,
                                 MP, tk_cache, lam=diff_lambda[j], sub_g=sub_g,
                                 lam_init=lam_init)
            x = matmul_residual(o, diff_w_o[j].astype(BF16), x, tm)
            dk_buf = emit_prompt_rows(kn, 0, W, j, n_diff, B, S, tm, True, dk_buf)
            dv_buf = emit_prompt_rows(p, 2, W, j, n_diff, B, S, tm, False, dv_buf)
            outs["dk_s"].append(kn[MP:].reshape(NS, T, diff_heads, 2, HEAD_DIM))
            outs["dv_s"].append(p[MP:, 2 * W:3 * W].reshape(NS, T, diff_heads, 2 * HEAD_DIM))

        x = peer_layer(x, norm_ffn_g[layer].reshape(1, D), layer, wq_t, sk_all, u_all, v_t_all,
                       tb)

    fk_p = jnp.transpose(fk_buf.reshape(n_fox, B, fox_heads, HEAD_DIM, S), (0, 1, 4, 2, 3))
    fv_p = jnp.transpose(fv_buf.reshape(n_fox, B, fox_heads, HEAD_DIM, S), (0, 1, 4, 2, 3))
    dk_p = jnp.transpose(dk_buf.reshape(n_diff, B, diff_heads, 2, HEAD_DIM, S),
                         (0, 1, 5, 2, 3, 4))
    dv_p = dv_buf.reshape(n_diff, B, S, diff_heads, 2 * HEAD_DIM)
    return (x[:MP].reshape(B, S, D), x[MP:].reshape(NS, T, D),
            fk_p, fv_p, jnp.stack(outs["fl_p"]), dk_p, dv_p,
            jnp.stack(outs["fk_s"]), jnp.stack(outs["fv_s"]), jnp.stack(outs["fl_s"]),
            jnp.stack(outs["dk_s"]), jnp.stack(outs["dv_s"]))


def kernel(x_prompt, x_sample, cache_fox_k, cache_fox_v, cache_fox_logf, cache_diff_k, cache_diff_v, norm_mix_g, norm_ffn_g, fox_w_in, fox_b_f, fox_q_g, fox_k_g, fox_w_o, diff_w_in, diff_q_g, diff_k_g, diff_lambda, diff_sub_g, diff_w_o, peer_w_q, peer_sub_keys, peer_u, peer_v):
    return _forward(x_prompt, x_sample, cache_fox_k, cache_fox_v, cache_fox_logf, cache_diff_k,
                    cache_diff_v, norm_mix_g, norm_ffn_g, fox_w_in, fox_b_f, fox_q_g, fox_k_g,
                    fox_w_o, diff_w_in, diff_q_g, diff_k_g, diff_lambda, diff_sub_g, diff_w_o,
                    peer_w_q, peer_sub_keys, peer_u, peer_v,
                    tm=512, tq=512, tk_cache=1024, tb=768)
```

```python
import functools
import math

import jax
import jax.numpy as jnp
from jax import lax
from jax.experimental import pallas as pl
from jax.experimental.pallas import tpu as pltpu

F32 = jnp.float32
BF16 = jnp.bfloat16
EPS = 1e-6
NEG_INF = -1e30
CHUNK = 64
ROPE_THETA = 10000.0
LOG2E = math.log2(math.e)
PEER_TOPK = 16
LANES = 128
SUBLANES = 8
MXU_WIDTH = 256
GATE_ROWS = 32
PIECE_ROWS = 1024
PAIRS_PER_STEP = 8
HEAD_DIM = 64
VMEM_LIMIT = 56 * 1024 * 1024

_NT = (((1,), (1,)), ((), ()))


def _params(sem, vmem=VMEM_LIMIT):
    return pltpu.CompilerParams(dimension_semantics=sem, vmem_limit_bytes=vmem)


def _norm_matmul_kernel(x_ref, g_ref, w_ref, o_ref, hn_ref):
    @pl.when(pl.program_id(1) == 0)
    def _():
        x = x_ref[...]
        y = x * lax.rsqrt(jnp.mean(x * x, axis=-1, keepdims=True) + EPS)
        hn_ref[...] = (y * g_ref[...]).astype(BF16)

    o_ref[...] = jnp.dot(hn_ref[...], w_ref[...], preferred_element_type=F32)


def norm_matmul(x, g, w, tm, tn):
    M, D = x.shape
    N = w.shape[1]
    return pl.pallas_call(
        _norm_matmul_kernel,
        grid=(M // tm, N // tn),
        in_specs=[pl.BlockSpec((tm, D), lambda i, j: (i, 0)),
                  pl.BlockSpec((1, D), lambda i, j: (0, 0)),
                  pl.BlockSpec((D, tn), lambda i, j: (0, j))],
        out_specs=pl.BlockSpec((tm, tn), lambda i, j: (i, j)),
        out_shape=jax.ShapeDtypeStruct((M, N), F32),
        scratch_shapes=[pltpu.VMEM((tm, D), BF16)],
        compiler_params=_params(("arbitrary", "arbitrary")),
        name="norm_matmul",
    )(x, g, w)


def _matmul_res_kernel(a_ref, w_ref, r_ref, o_ref):
    o_ref[...] = r_ref[...] + jnp.dot(a_ref[...].astype(BF16), w_ref[...],
                                      preferred_element_type=F32)


def matmul_residual(a, w, res, tm):
    M, K = a.shape
    N = w.shape[1]
    return pl.pallas_call(
        _matmul_res_kernel,
        grid=(M // tm,),
        in_specs=[pl.BlockSpec((tm, K), lambda i: (i, 0)),
                  pl.BlockSpec((K, N), lambda i: (0, 0)),
                  pl.BlockSpec((tm, N), lambda i: (i, 0))],
        out_specs=pl.BlockSpec((tm, N), lambda i: (i, 0)),
        out_shape=jax.ShapeDtypeStruct((M, N), F32),
        compiler_params=_params(("arbitrary",)),
        name="matmul_residual",
    )(a, w, res)


def _emit_kernel(src_ref, *rest, transpose):
    o_ref = rest[-1]
    o_ref[...] = src_ref[...].T if transpose else src_ref[...]


def emit_prompt_rows(src, colblk, width, layer, nlayers, batch, seq, tm, transpose, buf=None):
    nblk = seq // tm
    in_specs = [pl.BlockSpec((tm, width), lambda b, s: (b * nblk + s, colblk))]
    args = [src]
    if buf is not None:
        in_specs.append(pl.BlockSpec(memory_space=pl.ANY))
        args.append(buf)
    if transpose:
        out_shape = (nlayers, batch, width, seq)
        out_spec = pl.BlockSpec((None, None, width, tm), lambda b, s: (layer, b, 0, s))
    else:
        out_shape = (nlayers, batch, seq, width)
        out_spec = pl.BlockSpec((None, None, tm, width), lambda b, s: (layer, b, s, 0))
    return pl.pallas_call(
        functools.partial(_emit_kernel, transpose=transpose),
        grid=(batch, nblk),
        in_specs=in_specs,
        out_specs=out_spec,
        out_shape=jax.ShapeDtypeStruct(out_shape, F32),
        input_output_aliases={} if buf is None else {1: 0},
        compiler_params=_params(("arbitrary", "arbitrary")),
        name="emit_prompt_rows",
    )(*args)


def _head_rms64(xc, lo):
    sq = xc * xc
    s_lo = jnp.sum(jnp.where(lo, sq, 0.0), axis=-1, keepdims=True)
    s_hi = jnp.sum(jnp.where(lo, 0.0, sq), axis=-1, keepdims=True)
    ms = jnp.where(lo, s_lo, s_hi) * (1.0 / HEAD_DIM)
    return lax.rsqrt(ms + EPS)


def _fox_post_kernel(q_ref, k_ref, f_ref, bf_ref, qg_ref, kg_ref, qo_ref, ko_ref, lf_ref,
                     lft_ref):
    tm, width = q_ref.shape
    lo = lax.broadcasted_iota(jnp.int32, (tm, LANES), 1) < HEAD_DIM
    for src, g_ref, dst in ((q_ref, qg_ref, qo_ref), (k_ref, kg_ref, ko_ref)):
        for c in range(width // LANES):
            sl = slice(c * LANES, (c + 1) * LANES)
            xc = src[:, sl]
            dst[:, sl] = xc * _head_rms64(xc, lo) * g_ref[:, sl]
    z = f_ref[...] + bf_ref[...]
    lf = jnp.minimum(z, 0.0) - jnp.log1p(jnp.exp(-jnp.abs(z)))
    lf_ref[...] = lf
    lft_ref[...] = lf.T


def fox_post(p, b_f, q_g, k_g, tm):
    M = p.shape[0]
    W = q_g.shape[1]
    row = lambda c: pl.BlockSpec((tm, W), lambda i, c=c: (i, c))
    vec = lambda n: pl.BlockSpec((1, n), lambda i: (0, 0))
    return pl.pallas_call(
        _fox_post_kernel,
        grid=(M // tm,),
        in_specs=[row(0), row(1), pl.BlockSpec((tm, LANES), lambda i: (i, 4 * W // LANES)),
                  vec(LANES), vec(W), vec(W)],
        out_specs=[row(0), row(0), pl.BlockSpec((tm, LANES), lambda i: (i, 0)),
                   pl.BlockSpec((LANES, tm), lambda i: (0, i))],
        out_shape=[jax.ShapeDtypeStruct((M, W), F32), jax.ShapeDtypeStruct((M, W), F32),
                   jax.ShapeDtypeStruct((M, LANES), F32), jax.ShapeDtypeStruct((LANES, M), F32)],
        compiler_params=_params(("arbitrary",)),
        name="fox_post",
    )(p, p, p, b_f, q_g, k_g)


def _diff_post_kernel(q_ref, k_ref, cos_ref, sin_ref, qg_ref, kg_ref, qo_ref, ko_ref):
    tm, width = q_ref.shape
    lane = lax.broadcasted_iota(jnp.int32, (tm, LANES), 1)
    lo = lane < HEAD_DIM
    first_half = (lane % HEAD_DIM) < (HEAD_DIM // 2)
    cos = cos_ref[...]
    sin = sin_ref[...]
    for src, g_ref, dst in ((q_ref, qg_ref, qo_ref), (k_ref, kg_ref, ko_ref)):
        for c in range(width // LANES):
            sl = slice(c * LANES, (c + 1) * LANES)
            xc = src[:, sl]
            xn = xc * _head_rms64(xc, lo) * g_ref[:, sl]
            rot = jnp.where(first_half,
                            -pltpu.roll(xn, LANES - HEAD_DIM // 2, axis=1),
                            pltpu.roll(xn, HEAD_DIM // 2, axis=1))
            dst[:, sl] = xn * cos + rot * sin


def diff_post(p, cos, sin, q_g, k_g, tm):
    M = p.shape[0]
    W = q_g.shape[1]
    row = lambda c: pl.BlockSpec((tm, W), lambda i, c=c: (i, c))
    tab = pl.BlockSpec((tm, LANES), lambda i: (i, 0))
    vec = pl.BlockSpec((1, W), lambda i: (0, 0))
    return pl.pallas_call(
        _diff_post_kernel,
        grid=(M // tm,),
        in_specs=[row(0), row(1), tab, tab, vec, vec],
        out_specs=[row(0), row(0)],
        out_shape=[jax.ShapeDtypeStruct((M, W), F32), jax.ShapeDtypeStruct((M, W), F32)],
        compiler_params=_params(("arbitrary",)),
        name="diff_post",
    )(p, p, cos, sin, q_g, k_g)


def _cumsum_kernel(x_ref, o_ref):
    rows, length = x_ref.shape
    s_idx = lax.broadcasted_iota(jnp.int32, (LANES, LANES), 0)
    t_idx = lax.broadcasted_iota(jnp.int32, (LANES, LANES), 1)
    tri = (s_idx <= t_idx).astype(F32)
    carry = jnp.zeros((rows, 1), F32)
    for blk in range(length // LANES):
        sl = slice(blk * LANES, (blk + 1) * LANES)
        c = jnp.dot(x_ref[:, sl], tri, preferred_element_type=F32,
                    precision=lax.Precision.HIGHEST) + carry
        o_ref[:, sl] = c
        carry = c[:, LANES - 1:LANES]


def cumsum_lanes(x, rows, length):
    R, L = x.shape
    return pl.pallas_call(
        _cumsum_kernel,
        grid=(R // rows, L // length),
        in_specs=[pl.BlockSpec((rows, length), lambda i, j: (i, j))],
        out_specs=pl.BlockSpec((rows, length), lambda i, j: (i, j)),
        out_shape=jax.ShapeDtypeStruct((R, L), F32),
        compiler_params=_params(("arbitrary", "arbitrary")),
        name="cumsum_lanes",
    )(x)


def _diff_lambda(lam_ref, lam_init):
    lf = lam_ref[...]
    a = jnp.sum(lf[0:1, :] * lf[1:2, :], axis=-1, keepdims=True)
    b = jnp.sum(lf[2:3, :] * lf[3:4, :], axis=-1, keepdims=True)
    return jnp.exp(a) - jnp.exp(b) + lam_init


def _prompt_attn_kernel(qi_tab, ki_tab, q_ref, k_ref, v_ref, *rest, mode, lam_init, scale):
    if mode == "fox":
        cq_ref, ck_ref, gate_ref, o_ref = rest[:4]
        scratch = rest[4:]
    else:
        lam_ref, subg_ref, o_ref = rest[:3]
        scratch = rest[3:]
    qs_ref, m_ref, l_ref, acc_ref = scratch
    tq = q_ref.shape[0]
    tk = k_ref.shape[0]
    pairs = q_ref.shape[1] // LANES
    t = pl.program_id(2)
    qi = qi_tab[t]
    ki = ki_tab[t]

    @pl.when(ki == 0)
    def _():
        qs_ref[...] = (q_ref[...] * scale).astype(BF16)
        m_ref[...] = jnp.full(m_ref.shape, NEG_INF, F32)
        l_ref[...] = jnp.zeros(l_ref.shape, F32)
        acc_ref[...] = jnp.zeros(acc_ref.shape, F32)

    def update(s, vt, pp, mp):
        m_prev = m_ref[pp, mp]
        m_new = jnp.maximum(m_prev, jnp.max(s, axis=0, keepdims=True))
        alpha = jnp.exp2(m_prev - m_new)
        p = jnp.exp2(s - m_new)
        l_ref[pp, mp] = alpha * l_ref[pp, mp] + jnp.sum(p, axis=0, keepdims=True)
        acc_ref[pp, mp] = alpha * acc_ref[pp, mp] + jnp.dot(vt, p.astype(BF16),
                                                            preferred_element_type=F32)
        m_ref[pp, mp] = m_new

    def step(pp, diagonal):
        sl = slice(pp * LANES, (pp + 1) * LANES)
        k = k_ref[:, sl]
        lo_k = lax.broadcasted_iota(jnp.int32, (tk, LANES), 1) < HEAD_DIM
        k0 = jnp.where(lo_k, k, 0.0).astype(BF16)
        k1 = jnp.where(lo_k, 0.0, k).astype(BF16)
        qs = qs_ref[:, sl]
        s0 = lax.dot_general(k0, qs, _NT, preferred_element_type=F32)
        s1 = lax.dot_general(k1, qs, _NT, preferred_element_type=F32)
        if mode == "fox":
            s0 = s0 + cq_ref[pp, 0:1, :] - ck_ref[pp, :, 0:1]
            s1 = s1 + cq_ref[pp, 1:2, :] - ck_ref[pp, :, 1:2]
        if diagonal:
            key = lax.broadcasted_iota(jnp.int32, (tk, tq), 0)
            qry = lax.broadcasted_iota(jnp.int32, (tk, tq), 1)
            if mode == "fox":
                visible = key <= qry
            else:
                visible = (key // CHUNK) <= (qry // CHUNK)
            s0 = jnp.where(visible, s0, NEG_INF)
            s1 = jnp.where(visible, s1, NEG_INF)
        vt = v_ref[:, sl].T.astype(BF16)
        update(s0, vt, pp, 0)
        update(s1, vt, pp, 1)

    def finish(pp):
        sl = slice(pp * LANES, (pp + 1) * LANES)
        o0 = acc_ref[pp, 0] / l_ref[pp, 0]
        o1 = acc_ref[pp, 1] / l_ref[pp, 1]
        if mode == "fox":
            first_head = lax.broadcasted_iota(jnp.int32, (LANES, tq), 0) < HEAD_DIM
            ot = jnp.where(first_head, o0, o1)
            o_ref[:, sl] = ot.T * jax.nn.sigmoid(gate_ref[:, sl])
        else:
            ot = o0 - _diff_lambda(lam_ref, lam_init) * o1
            y = ot * lax.rsqrt(jnp.mean(ot * ot, axis=0, keepdims=True) + EPS)
            o_ref[:, sl] = (y.T * subg_ref[...]) * (1.0 - lam_init)

    @pl.when(ki < qi)
    def _():
        for pp in range(pairs):
            step(pp, False)

    @pl.when(ki == qi)
    def _():
        for pp in range(pairs):
            step(pp, True)
            finish(pp)


def prompt_attention(mode, q, k, vsrc, v_col0, batch, seq, tq, *, c_rows=None, c_cols=None,
                     gate_src=None, gate_col0=None, lam=None, sub_g=None, lam_init=0.0):
    W = q.shape[1]
    npair = W // LANES
    nq = seq // tq
    tri = [(a, b) for a in range(nq) for b in range(a + 1)]
    qi_tab = jnp.asarray([a for a, _ in tri], jnp.int32)
    ki_tab = jnp.asarray([b for _, b in tri], jnp.int32)
    pp = math.gcd(PAIRS_PER_STEP, npair)
    assert v_col0 % pp == 0 and (gate_col0 or 0) % pp == 0
    wide = pp * LANES
    qmap = lambda b, h, t, qt, kt: (b * nq + qt[t], h)
    kmap = lambda b, h, t, qt, kt: (b * nq + kt[t], h)
    in_specs = [pl.BlockSpec((tq, wide), qmap),
                pl.BlockSpec((tq, wide), kmap),
                pl.BlockSpec((tq, wide),
                             lambda b, h, t, qt, kt: (b * nq + kt[t], v_col0 // pp + h))]
    args = [q, k, vsrc]
    if mode == "fox":
        in_specs += [pl.BlockSpec((None, pp, 2, tq), lambda b, h, t, qt, kt: (b, h, 0, qt[t])),
                     pl.BlockSpec((None, pp, tq, 2), lambda b, h, t, qt, kt: (b, h, kt[t], 0)),
                     pl.BlockSpec((tq, wide),
                                  lambda b, h, t, qt, kt: (b * nq + qt[t], gate_col0 // pp + h))]
        args += [c_rows, c_cols, gate_src]
    else:
        in_specs += [pl.BlockSpec(lam.shape, lambda b, h, t, qt, kt: (0, 0)),
                     pl.BlockSpec((1, LANES), lambda b, h, t, qt, kt: (0, 0))]
        args += [lam, sub_g]
    grid_spec = pltpu.PrefetchScalarGridSpec(
        num_scalar_prefetch=2,
        grid=(batch, npair // pp, len(tri)),
        in_specs=in_specs,
        out_specs=pl.BlockSpec((tq, wide), qmap),
        scratch_shapes=[pltpu.VMEM((tq, wide), BF16),
                        pltpu.VMEM((pp, 2, 1, tq), F32), pltpu.VMEM((pp, 2, 1, tq), F32),
                        pltpu.VMEM((pp, 2, LANES, tq), F32)],
    )
    return pl.pallas_call(
        functools.partial(_prompt_attn_kernel, mode=mode, lam_init=lam_init,
                          scale=LOG2E / math.sqrt(HEAD_DIM)),
        grid_spec=grid_spec,
        out_shape=jax.ShapeDtypeStruct(q.shape, F32),
        compiler_params=_params(("arbitrary", "arbitrary", "arbitrary")),
        name=f"prompt_attention_{mode}",
    )(qi_tab, ki_tab, *args)


def _sample_attn_kernel(buf_ref, q_ref, kc_ref, vc_ref, kn_ref, vn_ref, *rest, mode, lam_init,
                        scale, past, v_time_minor):
    del buf_ref
    if mode == "fox":
        cq_ref, ckc_ref, ckn_ref, gate_ref, o_ref = rest[:5]
        scratch = rest[5:]
    else:
        lam_ref, subg_ref, o_ref = rest[:3]
        scratch = rest[3:]
    qbd_ref, m_ref, l_ref, acc_ref = scratch
    tnew, W = q_ref.shape
    nmap = W // HEAD_DIM
    R = nmap * tnew
    kstep = pl.program_id(1)
    nk = pl.num_programs(1) - 1

    @pl.when(kstep == 0)
    def _():
        q = q_ref[...] * scale
        qt = jnp.concatenate([q] * nmap, axis=0)
        rmap = lax.broadcasted_iota(jnp.int32, (R, W), 0) // tnew
        cmap = lax.broadcasted_iota(jnp.int32, (R, W), 1) // HEAD_DIM
        qbd_ref[...] = jnp.where(rmap == cmap, qt, 0.0).astype(BF16)
        m_ref[...] = jnp.full(m_ref.shape, NEG_INF, F32)
        l_ref[...] = jnp.zeros(l_ref.shape, F32)
        acc_ref[...] = jnp.zeros(acc_ref.shape, F32)

    def expand_rows(c):
        n = c.shape[1]
        return jnp.broadcast_to(c[:, None, :], (nmap, tnew, n)).reshape(R, n)

    def update(s, pv):
        m_prev = m_ref[...]
        m_new = jnp.maximum(m_prev, jnp.max(s, axis=-1, keepdims=True))
        alpha = jnp.exp(m_prev - m_new)
        p = jnp.exp(s - m_new)
        l_ref[...] = alpha * l_ref[...] + jnp.sum(p, axis=-1, keepdims=True)
        acc_ref[...] = alpha * acc_ref[...] + pv(p.astype(BF16))
        m_ref[...] = m_new

    @pl.when(kstep < nk)
    def _():
        s = jnp.dot(qbd_ref[...], kc_ref[...].astype(BF16), preferred_element_type=F32)
        if mode == "fox":
            s = s + cq_ref[...] - expand_rows(ckc_ref[...])
        if v_time_minor:
            v = vc_ref[...].astype(BF16)
            update(s, lambda p: lax.dot_general(p, v, _NT, preferred_element_type=F32))
        else:
            v = jnp.concatenate([vc_ref[:, h, :] for h in range(vc_ref.shape[1])],
                                axis=1).astype(BF16)
            update(s, lambda p: jnp.dot(p, v, preferred_element_type=F32))

    @pl.when(kstep == nk)
    def _():
        npad = LANES - tnew
        k = jnp.concatenate([kn_ref[...], jnp.zeros((npad, W), F32)], axis=0).astype(BF16)
        v = jnp.concatenate([vn_ref[...], jnp.zeros((npad, W), F32)], axis=0).astype(BF16)
        s = lax.dot_general(qbd_ref[...], k, _NT, preferred_element_type=F32)
        if mode == "fox":
            s = s + cq_ref[...] - expand_rows(ckn_ref[...])
        tpos = lax.broadcasted_iota(jnp.int32, (R, LANES), 0) % tnew
        spos = lax.broadcasted_iota(jnp.int32, (R, LANES), 1)
        if mode == "fox":
            visible = spos <= tpos
        else:
            visible = ((past + spos) // CHUNK) <= ((past + tpos) // CHUNK)
        visible = jnp.logical_and(visible, spos < tnew)
        update(jnp.where(visible, s, NEG_INF),
               lambda p: jnp.dot(p, v, preferred_element_type=F32))

        accn = acc_ref[...] / l_ref[...]
        if mode == "fox":
            lane_map = lax.broadcasted_iota(jnp.int32, (tnew, W), 1) // HEAD_DIM
            o = jnp.zeros((tnew, W), F32)
            for r in range(nmap):
                o = jnp.where(lane_map == r, accn[r * tnew:(r + 1) * tnew, :], o)
            o_ref[...] = o * jax.nn.sigmoid(gate_ref[...])
        else:
            lam = _diff_lambda(lam_ref, lam_init)
            for h in range(nmap // 2):
                sl = slice(h * LANES, (h + 1) * LANES)
                o = (accn[(2 * h) * tnew:(2 * h + 1) * tnew, sl]
                     - lam * accn[(2 * h + 1) * tnew:(2 * h + 2) * tnew, sl])
                y = o * lax.rsqrt(jnp.mean(o * o, axis=-1, keepdims=True) + EPS)
                o_ref[:, sl] = (y * subg_ref[...]) * (1.0 - lam_init)


def sample_attention(mode, out_buf, q, k, vsrc, v_colblk, kcache_t, vcache, v_time_minor,
                     layer_idx, nstream, tnew, row0, tk, *, cq=None, ck_cache=None, ck_new=None,
                     gate_src=None, gate_colblk=None, lam=None, sub_g=None, lam_init=0.0):
    W = q.shape[1]
    past = kcache_t.shape[2]
    nk = past // tk
    nmap = W // HEAD_DIM
    R = nmap * tnew
    rb0 = row0 // tnew
    wblk = lambda colblk: pl.BlockSpec((tnew, W), lambda b, s, c=colblk: (rb0 + b, c))
    time_minor = pl.BlockSpec((None, W, tk),
                              lambda b, s: (layer_idx * nstream + b, 0, jnp.minimum(s, nk - 1)))
    time_major = pl.BlockSpec((None, tk) + vcache.shape[2:],
                              lambda b, s: (layer_idx * nstream + b, jnp.minimum(s, nk - 1), 0, 0))
    in_specs = [wblk(0), time_minor, time_minor if v_time_minor else time_major,
                wblk(0), wblk(v_colblk)]
    args = [q, kcache_t, vcache, k, vsrc]
    if mode == "fox":
        in_specs += [pl.BlockSpec((None, R, 1), lambda b, s: (b, 0, 0)),
                     pl.BlockSpec((None, nmap, tk), lambda b, s: (b, 0, jnp.minimum(s, nk - 1))),
                     pl.BlockSpec((None, nmap, LANES), lambda b, s: (b, 0, 0)),
                     wblk(gate_colblk)]
        args += [cq, ck_cache, ck_new, gate_src]
    else:
        in_specs += [pl.BlockSpec(lam.shape, lambda b, s: (0, 0)),
                     pl.BlockSpec((1, LANES), lambda b, s: (0, 0))]
        args += [lam, sub_g]
    return pl.pallas_call(
        functools.partial(_sample_attn_kernel, mode=mode, lam_init=lam_init,
                          scale=1.0 / math.sqrt(HEAD_DIM), past=past, v_time_minor=v_time_minor),
        grid=(nstream, nk + 1),
        in_specs=[pl.BlockSpec(memory_space=pl.ANY)] + in_specs,
        out_specs=pl.BlockSpec((tnew, W), lambda b, s: (rb0 + b, 0)),
        out_shape=jax.ShapeDtypeStruct(out_buf.shape, F32),
        input_output_aliases={0: 0},
        scratch_shapes=[pltpu.VMEM((R, W), BF16), pltpu.VMEM((R, 1), F32),
                        pltpu.VMEM((R, 1), F32), pltpu.VMEM((R, W), F32)],
        compiler_params=_params(("arbitrary", "arbitrary")),
        name=f"sample_attention_{mode}",
    )(out_buf, *args)


def _top16_desc(e):
    rows = []
    for _ in range(PEER_TOPK):
        m = jnp.max(e, axis=0, keepdims=True)
        rows.append(m)
        e = jnp.where(e == m, -1.0, e)
    return jnp.concatenate(rows, axis=0)


def _compare_exchange(v, i, l):
    v[i], v[l] = jnp.maximum(v[i], v[l]), jnp.minimum(v[i], v[l])


def _top16_ranks(e):
    n = e.shape[0] // SUBLANES
    assert n == PEER_TOPK
    v = [e[r * SUBLANES:(r + 1) * SUBLANES, :] for r in range(n)]
    k = 2
    while k <= n:
        j = k // 2
        while j >= 1:
            for i in range(n):
                l = i ^ j
                if l > i:
                    if i & k == 0:
                        _compare_exchange(v, i, l)
                    else:
                        _compare_exchange(v, l, i)
            j //= 2
        k *= 2
    shift = SUBLANES // 2
    while shift >= 1:
        w = [pltpu.roll(x, shift, axis=0) for x in v]
        v = [jnp.maximum(v[i], w[n - 1 - i]) for i in range(n)]
        j = n // 2
        while j >= 1:
            for i in range(n):
                l = i ^ j
                if l > i:
                    _compare_exchange(v, i, l)
            j //= 2
        shift //= 2
    return v


def _ranks_on_sublanes(v, first):
    row = lax.broadcasted_iota(jnp.int32, v[0].shape, 0)
    out = v[first + SUBLANES - 1]
    for r in range(SUBLANES - 2, -1, -1):
        out = jnp.where(row == r, v[first + r], out)
    return out


def _pair_candidates(a, b):
    row = lax.broadcasted_iota(jnp.int32, a[0].shape, 0)
    b_lo = _ranks_on_sublanes(b, 0)
    b_hi = _ranks_on_sublanes(b, SUBLANES)
    parts = [b_lo * a[0], b_hi * a[0], b_lo * a[1]]
    for r1 in range(2, SUBLANES):
        parts.append(jnp.where(row < PEER_TOPK // (r1 + 1), b_lo * a[r1], -1.0))
    parts.append(b[0] * _ranks_on_sublanes(a, SUBLANES))
    return jnp.concatenate(parts, axis=0)


def _peer_kernel(x_ref, g_ref, wq_ref, sk_ref, u0_ref, un_ref, vp_ref, vl_ref, o_ref,
                 hn_ref, e1_ref, e2_ref, thr_ref, at0_ref, at1_ref, wt0_ref, wt1_ref, yt_ref,
                 *, nheads):
    tb, D = x_ref.shape
    te = un_ref.shape[0]
    nkeys = sk_ref.shape[1]
    half = sk_ref.shape[2]
    e_step = pl.program_id(1)
    ne = pl.num_programs(1)
    assert te == SUBLANES * nkeys
    chunk_w = min(MXU_WIDTH, tb)
    nchunk = tb // chunk_w

    @pl.when(e_step == 0)
    def _():
        x = x_ref[...]
        y = x * lax.rsqrt(jnp.mean(x * x, axis=-1, keepdims=True) + EPS)
        hn_ref[...] = (y * g_ref[...]).astype(BF16)

        def head_body(h, carry):
            es = []
            for c in range(2):
                r0 = pl.multiple_of((h * 2 + c) * half, half)
                qs = lax.dot_general(wq_ref[pl.ds(r0, half), :], hn_ref[...], _NT,
                                     preferred_element_type=F32).astype(BF16)
                s = jnp.dot(sk_ref[h * 2 + c], qs, preferred_element_type=F32)
                es.append(jnp.exp(s - jnp.max(s, axis=0, keepdims=True)))
            e2_ref[h] = es[1]
            tiny = jnp.finfo(F32).tiny
            for tc in range(tb // LANES):
                sl = slice(tc * LANES, (tc + 1) * LANES)
                e1 = es[0][:, sl]
                a = _top16_ranks(e1)
                b = _top16_ranks(es[1][:, sl])
                cand = _pair_candidates(a, b)
                top = jnp.maximum(_top16_desc(cand), 0.0)
                rz = 0.5 / jnp.sum(top, axis=0, keepdims=True)
                winner = cand >= jnp.maximum(top[PEER_TOPK - 1:PEER_TOPK, :], tiny)
                cand_n = _pair_candidates([ar * rz for ar in a], b)
                thr = jnp.min(jnp.where(winner, cand_n, 2.0), axis=0, keepdims=True)
                e1_ref[h, :, sl] = e1 * rz
                thr_ref[h, :, sl] = jnp.broadcast_to(jnp.maximum(thr, tiny), (SUBLANES, LANES))
            return carry

        def two_heads(it, carry):
            head_body(2 * it, carry)
            return head_body(2 * it + 1, carry)

        assert nheads % 2 == 0
        lax.fori_loop(0, nheads // 2, two_heads, 0)
        at0_ref[...] = lax.dot_general(u0_ref[...], hn_ref[...], _NT, preferred_element_type=F32)
        wt1_ref[...] = jnp.zeros(wt1_ref.shape, BF16)
        yt_ref[...] = jnp.zeros(yt_ref.shape, F32)

    pieces = ([("pre", r, c) for c in range(nchunk) for r in range(te // PIECE_ROWS)]
              + [("out", r, c) for c in range(nchunk) for r in range(D // PIECE_ROWS)])

    def mxu_piece(piece, at_nxt, wt_prv):
        kind, r, c = piece
        rows = slice(r * PIECE_ROWS, (r + 1) * PIECE_ROWS)
        cols = slice(c * chunk_w, (c + 1) * chunk_w)
        if kind == "pre":
            at_nxt[rows, cols] = lax.dot_general(un_ref[rows, :], hn_ref[cols, :], _NT,
                                                 preferred_element_type=F32)
        else:
            yt_ref[rows, cols] += jnp.dot(vp_ref[rows, :], wt_prv[:, cols],
                                          preferred_element_type=F32)

    def gate_unit(tc, ii, e1_tiles, at_cur, wt_cur):
        sl = slice(tc * LANES, (tc + 1) * LANES)
        for jh in range(nkeys // GATE_ROWS):
            js = slice(jh * GATE_ROWS, (jh + 1) * GATE_ROWS)
            gate = jnp.zeros((GATE_ROWS, LANES), F32)
            for h in range(nheads):
                t = e2_ref[h, js, sl] * e1_tiles[h][ii:ii + 1, :]
                gate = gate + jnp.where(t >= thr_ref[h, 0:1, sl], t, 0.0)
            rows = slice(ii * nkeys + jh * GATE_ROWS, ii * nkeys + (jh + 1) * GATE_ROWS)
            a = at_cur[rows, sl]
            act = a * (1.0 + lax.erf(a * (1.0 / math.sqrt(2.0))))
            wt_cur[rows, sl] = (gate * act).astype(BF16)

    def body(at_cur, at_nxt, wt_cur, wt_prv):
        i0 = pl.multiple_of(e_step * SUBLANES, SUBLANES)
        n_units = (tb // LANES) * SUBLANES
        done = 0
        for tc in range(tb // LANES):
            sl = slice(tc * LANES, (tc + 1) * LANES)
            e1_tiles = [e1_ref[h, pl.ds(i0, SUBLANES), sl] for h in range(nheads)]
            for ii in range(SUBLANES):
                unit = tc * SUBLANES + ii
                upto = min(len(pieces), unit * len(pieces) // n_units + 1)
                for piece in pieces[done:upto]:
                    mxu_piece(piece, at_nxt, wt_prv)
                done = upto
                gate_unit(tc, ii, e1_tiles, at_cur, wt_cur)

    @pl.when(e_step % 2 == 0)
    def _():
        body(at0_ref, at1_ref, wt0_ref, wt1_ref)

    @pl.when(e_step % 2 == 1)
    def _():
        body(at1_ref, at0_ref, wt1_ref, wt0_ref)

    @pl.when(e_step == ne - 1)
    def _():
        yt = yt_ref[...] + jnp.dot(vl_ref[...], wt1_ref[...], preferred_element_type=F32)
        o_ref[...] = x_ref[...] + yt.T


def peer_layer(x, g, layer, wq_t, sub_keys, u, v_t, tb):
    M, D = x.shape
    E = u.shape[1]
    nheads = sub_keys.shape[1] // 2
    nkeys = sub_keys.shape[2]
    te = SUBLANES * nkeys
    ne = E // te
    assert ne % 2 == 0
    return pl.pallas_call(
        functools.partial(_peer_kernel, nheads=nheads),
        grid=(M // tb, ne),
        in_specs=[pl.BlockSpec((tb, D), lambda i, e: (i, 0)),
                  pl.BlockSpec((1, D), lambda i, e: (0, 0)),
                  pl.BlockSpec((None,) + wq_t.shape[1:], lambda i, e: (layer, 0, 0)),
                  pl.BlockSpec((None,) + sub_keys.shape[1:], lambda i, e: (layer, 0, 0, 0)),
                  pl.BlockSpec((None, te, D), lambda i, e: (layer, 0, 0)),
                  pl.BlockSpec((None, te, D),
                               lambda i, e: (layer, jnp.minimum(e + 1, ne - 1), 0)),
                  pl.BlockSpec((None, D, te), lambda i, e: (layer, 0, jnp.maximum(e - 1, 0))),
                  pl.BlockSpec((None, D, te), lambda i, e: (layer, 0, ne - 1))],
        out_specs=pl.BlockSpec((tb, D), lambda i, e: (i, 0)),
        out_shape=jax.ShapeDtypeStruct((M, D), F32),
        scratch_shapes=[pltpu.VMEM((tb, D), BF16),
                        pltpu.VMEM((nheads, nkeys, tb), F32),
                        pltpu.VMEM((nheads, nkeys, tb), F32),
                        pltpu.VMEM((nheads, SUBLANES, tb), F32),
                        pltpu.VMEM((te, tb), F32),
                        pltpu.VMEM((te, tb), F32),
                        pltpu.VMEM((te, tb), BF16),
                        pltpu.VMEM((te, tb), BF16),
                        pltpu.VMEM((D, tb), F32)],
        compiler_params=_params(("arbitrary", "arbitrary")),
        name="peer",
    )(x, g, wq_t, sub_keys, u, u, v_t, v_t)


def _rope_tables(pos):
    inv = ROPE_THETA ** (-jnp.arange(0, HEAD_DIM, 2, dtype=F32) / HEAD_DIM)
    ang = pos.astype(F32)[:, None] * inv[None, :]
    ang = jnp.concatenate([ang, ang, ang, ang], axis=-1)
    return jnp.cos(ang), jnp.sin(ang)


def _tile_heads(g, width):
    return jnp.tile(g, width // g.shape[0]).reshape(1, width)


def _time_minor(cache):
    nd = cache.ndim
    c = jnp.transpose(cache, (0, 1) + tuple(range(3, nd)) + (2,))
    return c.reshape(c.shape[0] * c.shape[1], -1, c.shape[-1])


def _forward(x_prompt, x_sample, cache_fox_k, cache_fox_v, cache_fox_logf, cache_diff_k,
             cache_diff_v, norm_mix_g, norm_ffn_g, fox_w_in, fox_b_f, fox_q_g, fox_k_g, fox_w_o,
             diff_w_in, diff_q_g, diff_k_g, diff_lambda, diff_sub_g, diff_w_o,
             peer_w_q, peer_sub_keys, peer_u, peer_v, *, tm, tq, tk_cache, tb):
    B, S, D = x_prompt.shape
    NS, T, _ = x_sample.shape
    depth = norm_mix_g.shape[0]
    past = cache_fox_k.shape[2]
    MP = B * S
    M = MP + NS * T
    fox_heads = cache_fox_k.shape[3]
    diff_heads = cache_diff_k.shape[3]
    W = fox_heads * HEAD_DIM
    npair = W // LANES

    x = jnp.concatenate([x_prompt.reshape(MP, D), x_sample.reshape(NS * T, D)], axis=0)

    pos = jnp.concatenate([jnp.tile(jnp.arange(S, dtype=jnp.int32), B),
                           jnp.tile(past + jnp.arange(T, dtype=jnp.int32), NS)])
    cos, sin = _rope_tables(pos)

    kc_fox = _time_minor(cache_fox_k)
    vc_fox = _time_minor(cache_fox_v)
    kc_diff = _time_minor(cache_diff_k)
    vc_diff = cache_diff_v.reshape((-1, past) + cache_diff_v.shape[3:])
    lf_cache = jnp.swapaxes(cache_fox_logf, 2, 3)

    wq_t = jnp.swapaxes(peer_w_q, 1, 2).astype(BF16)
    sk_all = peer_sub_keys.reshape((depth, -1) + peer_sub_keys.shape[3:]).astype(BF16)
    u_all = peer_u.astype(BF16)
    v_t_all = jnp.swapaxes(peer_v, 1, 2).astype(BF16)

    n_fox, n_diff = cache_fox_k.shape[0], cache_diff_k.shape[0]
    outs = {k: [] for k in ("fl_p", "fk_s", "fv_s", "fl_s", "dk_s", "dv_s")}
    fk_buf = fv_buf = dk_buf = dv_buf = None

    for layer in range(depth):
        j = layer // 2
        g_mix = norm_mix_g[layer].reshape(1, D)
        if layer % 2 == 0:
            w_in = jnp.pad(fox_w_in[j], ((0, 0), (0, LANES - fox_heads))).astype(BF16)
            p = norm_matmul(x, g_mix, w_in, tm, w_in.shape[1])
            b_f = jnp.pad(fox_b_f[j], (0, LANES - fox_heads)).reshape(1, LANES)
            qn, kn, lf, lf_t = fox_post(p, b_f, _tile_heads(fox_q_g[j], W),
                                        _tile_heads(fox_k_g[j], W), tm)
            lf = lf[:, :fox_heads]
            lf_p = lf[:MP].reshape(B, S, fox_heads)
            lf_s = lf[MP:].reshape(NS, T, fox_heads)
            c_t = cumsum_lanes(lf_t[:, :MP], LANES, S)[:fox_heads]
            c_pt = jnp.swapaxes(c_t.reshape(fox_heads, B, S), 0, 1)
            c_rows = (c_pt * LOG2E).reshape(B, npair, 2, S)
            c_cols = jnp.swapaxes(c_rows, 2, 3)
            o_p = prompt_attention("fox", qn, kn, p, 2 * npair, B, S, tq,
                                   c_rows=c_rows, c_cols=c_cols, gate_src=p,
                                   gate_col0=3 * npair)
            c_cache = cumsum_lanes(lf_cache[j].reshape(NS * fox_heads, past), LANES, past)
            lf_new = jnp.pad(jnp.swapaxes(lf_s, 1, 2), ((0, 0), (0, 0), (0, LANES - T)))
            c_new = cumsum_lanes(lf_new.reshape(NS * fox_heads, LANES), LANES, LANES)
            ck_new = (c_cache[:, past - 1:] + c_new).reshape(NS, fox_heads, LANES)
            cq_s = ck_new[:, :, :T].reshape(NS, fox_heads * T, 1)
            o = sample_attention("fox", o_p, qn, kn, p, 2, kc_fox, vc_fox, True, j, NS, T, MP,
                                 tk_cache, cq=cq_s,
                                 ck_cache=c_cache.reshape(NS, fox_heads, past), ck_new=ck_new,
                                 gate_src=p, gate_colblk=3)
            x = matmul_residual(o, fox_w_o[j].astype(BF16), x, tm)
            fk_buf = emit_prompt_rows(kn, 0, W, j, n_fox, B, S, tm, True, fk_buf)
            fv_buf = emit_prompt_rows(p, 2, W, j, n_fox, B, S, tm, True, fv_buf)
            outs["fl_p"].append(lf_p)
            outs["fk_s"].append(kn[MP:].reshape(NS, T, fox_heads, HEAD_DIM))
            outs["fv_s"].append(p[MP:, 2 * W:3 * W].reshape(NS, T, fox_heads, HEAD_DIM))
            outs["fl_s"].append(lf_s)
        else:
            lam_init = 0.8 - 0.6 * math.exp(-0.3 * layer)
            p = norm_matmul(x, g_mix, diff_w_in[j].astype(BF16), tm, 3 * W)
            qn, kn = diff_post(p, cos, sin, _tile_heads(diff_q_g[j], W),
                               _tile_heads(diff_k_g[j], W), tm)
            sub_g = diff_sub_g[j].reshape(1, LANES)
            o_p = prompt_attention("diff", qn, kn, p, 2 * npair, B, S, tq,
                                   lam=diff_lambda[j], sub_g=sub_g, lam_init=lam_init)
            o = sample_attention("diff", o_p, qn, kn, p, 2, kc_diff, vc_diff, False, j, NS, T,
                                 MP, tk_cache, lam=diff_lambda[j], sub_g=sub_g,
                                 lam_init=lam_init)
            x = matmul_residual(o, diff_w_o[j].astype(BF16), x, tm)
            dk_buf = emit_prompt_rows(kn, 0, W, j, n_diff, B, S, tm, True, dk_buf)
            dv_buf = emit_prompt_rows(p, 2, W, j, n_diff, B, S, tm, False, dv_buf)
            outs["dk_s"].append(kn[MP:].reshape(NS, T, diff_heads, 2, HEAD_DIM))
            outs["dv_s"].append(p[MP:, 2 * W:3 * W].reshape(NS, T, diff_heads, 2 * HEAD_DIM))

        x = peer_layer(x, norm_ffn_g[layer].reshape(1, D), layer, wq_t, sk_all, u_all, v_t_all,
                       tb)

    fk_p = jnp.transpose(fk_buf.reshape(n_fox, B, fox_heads, HEAD_DIM, S), (0, 1, 4, 2, 3))
    fv_p = jnp.transpose(fv_buf.reshape(n_fox, B, fox_heads, HEAD_DIM, S), (0, 1, 4, 2, 3))
    dk_p = jnp.transpose(dk_buf.reshape(n_diff, B, diff_heads, 2, HEAD_DIM, S),
                         (0, 1, 5, 2, 3, 4))
    dv_p = dv_buf.reshape(n_diff, B, S, diff_heads, 2 * HEAD_DIM)
    return (x[:MP].reshape(B, S, D), x[MP:].reshape(NS, T, D),
            fk_p, fv_p, jnp.stack(outs["fl_p"]), dk_p, dv_p,
            jnp.stack(outs["fk_s"]), jnp.stack(outs["fv_s"]), jnp.stack(outs["fl_s"]),
            jnp.stack(outs["dk_s"]), jnp.stack(outs["dv_s"]))


def kernel(x_prompt, x_sample, cache_fox_k, cache_fox_v, cache_fox_logf, cache_diff_k, cache_diff_v, norm_mix_g, norm_ffn_g, fox_w_in, fox_b_f, fox_q_g, fox_k_g, fox_w_o, diff_w_in, diff_q_g, diff_k_g, diff_lambda, diff_sub_g, diff_w_o, peer_w_q, peer_sub_keys, peer_u, peer_v):
    return _forward(x_prompt, x_sample, cache_fox_k, cache_fox_v, cache_fox_logf, cache_diff_k,
                    cache_diff_v, norm_mix_g, norm_ffn_g, fox_w_in, fox_b_f, fox_q_g, fox_k_g,
                    fox_w_o, diff_w_in, diff_q_g, diff_k_g, diff_lambda, diff_sub_g, diff_w_o,
                    peer_w_q, peer_sub_keys, peer_u, peer_v,
                    tm=512, tq=512, tk_cache=1024, tb=768)
```
